```python
import jax, jax.numpy as jnp
from jax import lax
import numpy as np

D_MODEL = 2048
BATCH = 4
SEQ = 4096
DEPTH = 1

CONV_WIDTH = 3
CONV_GROUPS = 8
D_CONV = D_MODEL // 2
HGRN_HEADS = 8
HGRN_DK = 128
HGRN_DV = (D_MODEL // 2) // HGRN_HEADS
D_HGRN_K = HGRN_HEADS * HGRN_DK
D_HGRN_V = HGRN_HEADS * HGRN_DV
CHUNK = 64
D_FF = ((8 * D_MODEL // 3 + 255) // 256) * 256
N_MOD = 6
EPS = 1e-6
SPLITS = (D_CONV, D_CONV, D_CONV, D_HGRN_K, D_HGRN_K, D_HGRN_V, D_HGRN_V, D_MODEL, D_MODEL)
D_IN = sum(SPLITS)

kernel_name = "hybrid_conv_hgrn2_gated_merge_block"


def rmsnorm(x, g):
    xf = x.astype(jnp.float32)
    y = xf * lax.rsqrt(jnp.mean(xf * xf, axis=-1, keepdims=True) + EPS)
    return (y * g.astype(jnp.float32)).astype(x.dtype)


def causal_depthwise_conv(u, w):
    s = u.shape[1]
    upad = jnp.pad(u, ((0, 0), (CONV_WIDTH - 1, 0), (0, 0)))
    y = w[0] * upad[:, 0:s]
    for k in range(1, CONV_WIDTH):
        y = y + w[k] * upad[:, k:k + s]
    return y


def to_chunks(t):
    b, s, h, d = t.shape
    return t.reshape(b, s // CHUNK, CHUNK, h, d).transpose(1, 0, 3, 2, 4)


def from_chunks(t):
    n, b, h, c, d = t.shape
    return t.transpose(1, 0, 3, 2, 4).reshape(b, n * c, h, d)


def hgrn2_chunked(q, k, v, log_f):
    qc, kc, vc = to_chunks(q), to_chunks(k), to_chunks(v)
    bc = jnp.cumsum(to_chunks(log_f), axis=3)
    causal = jnp.tril(jnp.ones((CHUNK, CHUNK), dtype=bool))[:, :, None]
    b_, h_ = q.shape[0], q.shape[2]
    s0 = jnp.zeros((b_, h_, HGRN_DK, HGRN_DV), jnp.float32)

    def step(state, xs):
        qi, ki, vi, bi = xs
        b_last = bi[:, :, -1:, :]
        o_inter = jnp.einsum('bhtk,bhkv->bhtv', qi * jnp.exp(bi), state)
        diff = bi[:, :, :, None, :] - bi[:, :, None, :, :]
        decay = jnp.exp(jnp.where(causal, diff, -jnp.inf))
        scores = jnp.einsum('bhtk,bhtsk,bhsk->bhts', qi, decay, ki)
        o_intra = jnp.einsum('bhts,bhsv->bhtv', scores, vi)
        new_state = (jnp.swapaxes(jnp.exp(b_last), -1, -2) * state
                     + jnp.einsum('bhsk,bhsv->bhkv', ki * jnp.exp(b_last - bi), vi))
        return new_state, o_inter + o_intra

    _, o = lax.scan(step, s0, (qc, kc, vc, bc))
    return from_chunks(o)


def setup_inputs(seed: int = 0) -> dict:
    key = jax.random.key(seed)
    ks = jax.random.split(key, 20)

    def nrm(k, shape, fan_in):
        return jax.random.normal(k, shape, jnp.float32) * (fan_in ** -0.5)

    def gain(k, shape):
        return 1.0 + 0.02 * jax.random.normal(k, shape, jnp.float32)

    return {
        "x": jax.random.normal(ks[0], (BATCH, SEQ, D_MODEL), jnp.float32),
        "c": jax.random.normal(ks[1], (BATCH, D_MODEL), jnp.float32),
        "w_ada": nrm(ks[2], (DEPTH, D_MODEL, N_MOD * D_MODEL), D_MODEL) * 0.5,
        "b_ada": 0.02 * jax.random.normal(ks[3], (DEPTH, N_MOD * D_MODEL), jnp.float32),
        "norm_mix_g": gain(ks[4], (DEPTH, D_MODEL)),
        "w_in": nrm(ks[5], (DEPTH, D_MODEL, D_IN), D_MODEL),
        "conv_w": nrm(ks[6], (DEPTH, CONV_WIDTH, D_CONV), CONV_WIDTH),
        "lb_param": jax.random.normal(ks[7], (DEPTH + 1, D_HGRN_K), jnp.float32),
        "gnorm_g": gain(ks[8], (DEPTH, HGRN_DV)),
        "w_conv_out": nrm(ks[9], (DEPTH, D_CONV, D_MODEL), D_CONV),
        "w_hgrn_out": nrm(ks[10], (DEPTH, D_HGRN_V, D_MODEL), D_HGRN_V),
        "w_o": nrm(ks[11], (DEPTH, D_MODEL, D_MODEL), D_MODEL),
        "norm_ffn_g": gain(ks[12], (DEPTH, D_MODEL)),
        "w_ffn_gate": nrm(ks[13], (DEPTH, D_MODEL, D_FF), D_MODEL),
        "w_ffn_up": nrm(ks[14], (DEPTH, D_MODEL, D_FF), D_MODEL),
        "w_ffn_down": nrm(ks[15], (DEPTH, D_FF, D_MODEL), D_FF),
        "norm_final_g": gain(ks[16], (D_MODEL,)),
    }


def reference(x, c, w_ada, b_ada, norm_mix_g, w_in, conv_w, lb_param, gnorm_g,
              w_conv_out, w_hgrn_out, w_o, norm_ffn_g, w_ffn_gate, w_ffn_up, w_ffn_down,
              norm_final_g):
    b, s, _ = x.shape
    lb_all = jnp.cumsum(jax.nn.softmax(lb_param.astype(jnp.float32), axis=0), axis=0)
    split_idx = list(np.cumsum(SPLITS)[:-1])
    c_act = jax.nn.silu(c)

    for l in range(DEPTH):
        mod = c_act @ w_ada[l] + b_ada[l]
        sh_m, sc_m, gt_m, sh_f, sc_f, gt_f = [m[:, None, :] for m in jnp.split(mod, N_MOD, axis=-1)]

        h = rmsnorm(x, norm_mix_g[l]) * (1.0 + sc_m) + sh_m
        proj = h @ w_in[l]
        a_b, a_c, a_x, q, f_logit, i_in, g_out, gate_a, gate_b = jnp.split(proj, split_idx, axis=-1)

        y_a = (a_b * causal_depthwise_conv(a_c * a_x, conv_w[l])) @ w_conv_out[l]

        lb = lb_all[l]
        f = lb + (1.0 - lb) * jax.nn.sigmoid(f_logit.astype(jnp.float32))
        qh = jax.nn.silu(q.astype(jnp.float32)).reshape(b, s, HGRN_HEADS, HGRN_DK)
        kh = (1.0 - f).reshape(b, s, HGRN_HEADS, HGRN_DK)
        log_f = jnp.log(f).reshape(b, s, HGRN_HEADS, HGRN_DK)
        vh = i_in.astype(jnp.float32).reshape(b, s, HGRN_HEADS, HGRN_DV)
        o = hgrn2_chunked(qh, kh, vh, log_f)
        o = rmsnorm(o, gnorm_g[l]) * jax.nn.silu(g_out.astype(jnp.float32)).reshape(b, s, HGRN_HEADS, HGRN_DV)
        y_b = o.reshape(b, s, D_HGRN_V).astype(x.dtype) @ w_hgrn_out[l]

        merged = jax.nn.sigmoid(gate_a) * y_a + jax.nn.sigmoid(gate_b) * y_b
        x = x + gt_m * (merged @ w_o[l])

        h2 = rmsnorm(x, norm_ffn_g[l]) * (1.0 + sc_f) + sh_f
        ff = (jax.nn.silu(h2 @ w_ffn_gate[l]) * (h2 @ w_ffn_up[l])) @ w_ffn_down[l]
        x = x + gt_f * ff

    return rmsnorm(x, norm_final_g)
```

```python
import functools

import jax
import jax.numpy as jnp
import numpy as np
from jax import lax
from jax.experimental import pallas as pl
from jax.experimental.pallas import tpu as pltpu

F32 = jnp.float32
BF16 = jnp.bfloat16

EPS = 1e-6
CONV_WIDTH = 3
HEADS = 8
HEAD_DIM = 128
CHUNK = 128
LEVELS = 8
SUBLANES = 8
VMEM_LIMIT = 56 * 1024 * 1024


def _cparams(*sem):
    return pltpu.CompilerParams(dimension_semantics=sem, vmem_limit_bytes=VMEM_LIMIT)


def _dot(a, b):
    return jnp.dot(a, b, preferred_element_type=F32)


def _dot_nt(a, b):
    return lax.dot_general(a, b, (((1,), (1,)), ((), ())), preferred_element_type=F32)


def _silu(x):
    return x * jax.nn.sigmoid(x)


def _rms(x):
    return x * lax.rsqrt(jnp.mean(x * x, axis=-1, keepdims=True) + EPS)


def _mod_kernel(c_ref, w_ref, b_ref, o_ref):
    c = c_ref[...]
    o_ref[...] = jnp.dot(_silu(c), w_ref[...], preferred_element_type=F32,
                         precision=lax.Precision.HIGHEST) + b_ref[...]


def _modulation(c, w_ada, b_ada):
    b, d = c.shape
    n = w_ada.shape[1]
    tn = 1024
    return pl.pallas_call(
        _mod_kernel,
        grid=(n // tn,),
        in_specs=[pl.BlockSpec((b, d), lambda j: (0, 0)),
                  pl.BlockSpec((d, tn), lambda j: (0, j)),
                  pl.BlockSpec((1, tn), lambda j: (0, j))],
        out_specs=pl.BlockSpec((b, tn), lambda j: (0, j)),
        out_shape=jax.ShapeDtypeStruct((b, n), F32),
        compiler_params=_cparams("parallel"),
        name="mod",
    )(c, w_ada, b_ada.reshape(1, n))


def _hnorm_kernel(x_ref, g_ref, sc_ref, sh_ref, o_ref):
    y = _rms(x_ref[...]) * g_ref[...]
    o_ref[...] = (y * (1.0 + sc_ref[...]) + sh_ref[...]).astype(BF16)


def _hnorm(x, g, sc, sh):
    b, s, d = x.shape
    tr = 512
    vec = pl.BlockSpec((None, 1, d), lambda bi, r: (bi, 0, 0))
    return pl.pallas_call(
        _hnorm_kernel,
        grid=(b, s // tr),
        in_specs=[pl.BlockSpec((None, tr, d), lambda bi, r: (bi, r, 0)),
                  pl.BlockSpec((1, d), lambda bi, r: (0, 0)), vec, vec],
        out_specs=pl.BlockSpec((None, tr, d), lambda bi, r: (bi, r, 0)),
        out_shape=jax.ShapeDtypeStruct((b, s, d), BF16),
        compiler_params=_cparams("parallel", "parallel"),
        name="hnorm",
    )(x, g.reshape(1, d), sc, sh)


def _proj_call(body, h, w, extras, extra_specs, out_widths, out_dtypes, tm, tn, name,
               scratch_shapes=()):
    t, d = h.shape
    n = w.shape[1]
    nj = n // tn
    in_specs = [pl.BlockSpec((tm, d), lambda j, i: (i, 0)),
                pl.BlockSpec((d, tn), lambda j, i: (0, j))] + list(extra_specs)
    out_specs = [pl.BlockSpec((tm, ow), lambda j, i: (i, j)) for ow in out_widths]
    out_shape = [jax.ShapeDtypeStruct((t, nj * ow), dt) for ow, dt in zip(out_widths, out_dtypes)]
    return pl.pallas_call(
        body,
        grid=(nj, t // tm),
        in_specs=in_specs,
        out_specs=out_specs,
        out_shape=out_shape,
        scratch_shapes=list(scratch_shapes),
        compiler_params=_cparams("arbitrary", "arbitrary"),
        name=name,
    )(h, w, *extras)


def _conv_body(h_ref, w_ref, cw_ref, o_ref, ubuf, *, tm, cw, tiles_per_seq):
    i = pl.program_id(1)
    acc = _dot(h_ref[...], w_ref[...])
    u = acc[:, cw:2 * cw] * acc[:, 2 * cw:]

    @pl.when(i % tiles_per_seq == 0)
    def _():
        ubuf[0:SUBLANES, :] = jnp.zeros((SUBLANES, cw), F32)

    ubuf[SUBLANES:SUBLANES + tm, :] = u
    wk = cw_ref[...]
    y = wk[2:3, :] * u
    y = y + wk[1:2, :] * ubuf[SUBLANES - 1:SUBLANES - 1 + tm, :]
    y = y + wk[0:1, :] * ubuf[SUBLANES - 2:SUBLANES - 2 + tm, :]
    o_ref[...] = (acc[:, :cw] * y).astype(BF16)
    ubuf[0:SUBLANES, :] = ubuf[tm:tm + SUBLANES, :]


def _silu_body(h_ref, w_ref, o_ref):
    o_ref[...] = _silu(_dot(h_ref[...], w_ref[...])).astype(BF16)


def _ident_body(h_ref, w_ref, o_ref):
    o_ref[...] = _dot(h_ref[...], w_ref[...]).astype(BF16)


def _sigmoid_body(h_ref, w_ref, o_ref):
    o_ref[...] = jax.nn.sigmoid(_dot(h_ref[...], w_ref[...])).astype(BF16)


def _forget_body(h_ref, w_ref, lbp_ref, k_ref, lf_ref, *, layer):
    p = lbp_ref[...]
    e = jnp.exp(p - jnp.max(p, axis=0, keepdims=True))
    sm = e / jnp.sum(e, axis=0, keepdims=True)
    lb = jnp.sum(sm[:layer + 1, :], axis=0, keepdims=True)
    f = lb + (1.0 - lb) * jax.nn.sigmoid(_dot(h_ref[...], w_ref[...]))
    k_ref[...] = (1.0 - f).astype(BF16)
    lf_ref[...] = jnp.log(f)


def _level_tables():
    n = CHUNK
    level = -np.ones((n, n), np.int32)
    expo = np.zeros((LEVELS + 1, n, n), np.float32)
    for t in range(n):
        level[t, t] = 0
        for s in range(t):
            level[t, s] = (t ^ s).bit_length()
        for li in range(1, LEVELS):
            m = 1 << (li - 1)
            blk = (t // m) * m
            if (t % (2 * m)) >= m:
                expo[li - 1, t, blk:t + 1] = 1.0
            else:
                expo[li - 1, t, t + 1:blk + m] = 1.0
        expo[LEVELS - 1, t, :t + 1] = 1.0
        expo[LEVELS, t, t + 1:] = 1.0
    return level, expo.reshape((LEVELS + 1) * n, n)


def _hgrn_kernel(q_ref, k_ref, lf_ref, v_ref, gs_ref, gn_ref, lv_ref, ex_ref, o_ref, st_ref):
    c = pl.program_id(1)

    @pl.when(c == 0)
    def _():
        st_ref[...] = jnp.zeros_like(st_ref)

    lv = lv_ref[...]
    ex = ex_ref[...]
    gn = gn_ref[...]
    n = CHUNK
    for hd in range(HEADS):
        sl = slice(hd * HEAD_DIM, (hd + 1) * HEAD_DIM)
        lf = lf_ref[:, sl]
        hi = lf.astype(BF16)
        lo = (lf - hi.astype(F32)).astype(BF16)
        dec = _dot(ex, hi) + _dot(ex, lo)
        qb = q_ref[:, sl]
        kb = k_ref[:, sl]
        vb = v_ref[:, sl]
        q = qb.astype(F32)
        k = kb.astype(F32)
        scores = jnp.where(lv == 0, _dot_nt(qb, kb), 0.0)
        for li in range(1, LEVELS):
            e = jnp.exp(dec[(li - 1) * n:li * n, :])
            p = _dot_nt((q * e).astype(BF16), (k * e).astype(BF16))
            scores = jnp.where(lv == li, p, scores)
        g_in = dec[(LEVELS - 1) * n:LEVELS * n, :]
        q_in = (q * jnp.exp(g_in)).astype(BF16)
        k_out = (k * jnp.exp(dec[LEVELS * n:, :])).astype(BF16)
        st = st_ref[hd]
        o = _dot_nt(q_in, st.astype(BF16)) + _dot(scores.astype(BF16), vb)
        vt = vb.astype(F32).T.astype(BF16)
        st_ref[hd] = st * jnp.exp(g_in[n - 1:n, :]) + _dot(vt, k_out)
        y = _rms(o) * gn * gs_ref[:, sl].astype(F32)
        o_ref[:, sl] = y.astype(BF16)


def _hgrn(q, k, lf, v, gs, gnorm, batch, seq):
    t, dk = q.shape
    nc = seq // CHUNK
    level, expo = _level_tables()
    row = lambda b, c: (b * nc + c, 0)
    const = lambda b, c: (0, 0)
    blk = pl.BlockSpec((CHUNK, dk), row)
    return pl.pallas_call(
        _hgrn_kernel,
        grid=(batch, nc),
        in_specs=[blk, blk, blk, blk, blk,
                  pl.BlockSpec((1, HEAD_DIM), const),
                  pl.BlockSpec((CHUNK, CHUNK), const),
                  pl.BlockSpec(((LEVELS + 1) * CHUNK, CHUNK), const)],
        out_specs=blk,
        out_shape=jax.ShapeDtypeStruct((t, dk), BF16),
        scratch_shapes=[pltpu.VMEM((HEADS, HEAD_DIM, HEAD_DIM), F32)],
        compiler_params=_cparams("arbitrary", "arbitrary"),
        name="hgrn",
    )(q, k, lf, v, gs, gnorm.reshape(1, HEAD_DIM), jnp.asarray(level), jnp.asarray(expo, BF16))


def _mix_kernel(z_ref, on_ref, sg_ref, x_ref, wc_ref, wh_ref, wo_ref, gt_ref, g2_ref, sc_ref,
                sh_ref, x1_ref, h2_ref, *, d):
    y_a = _dot(z_ref[...], wc_ref[...])
    y_b = _dot(on_ref[...], wh_ref[...])
    merged = sg_ref[:, :d].astype(F32) * y_a + sg_ref[:, d:].astype(F32) * y_b
    x1 = x_ref[...] + gt_ref[...] * _dot(merged.astype(BF16), wo_ref[...])
    x1_ref[...] = x1
    h2_ref[...] = (_rms(x1) * g2_ref[...] * (1.0 + sc_ref[...]) + sh_ref[...]).astype(BF16)


def _mix(z, on, sg, x, wc, wh, wo, gt, g2, sc, sh, seq):
    t, d = x.shape
    tm = 256
    per = seq // tm
    row = lambda i: (i, 0)
    const = lambda i: (0, 0)
    vec = pl.BlockSpec((None, 1, d), lambda i: (i // per, 0, 0))
    resident = lambda shape: pl.BlockSpec(shape, const, pipeline_mode=pl.Buffered(1))
    return pl.pallas_call(
        functools.partial(_mix_kernel, d=d),
        grid=(t // tm,),
        in_specs=[pl.BlockSpec((tm, z.shape[1]), row),
                  pl.BlockSpec((tm, on.shape[1]), row),
                  pl.BlockSpec((tm, 2 * d), row),
                  pl.BlockSpec((tm, d), row),
                  resident(wc.shape), resident(wh.shape), resident(wo.shape),
                  vec, pl.BlockSpec((1, d), const), vec, vec],
        out_specs=[pl.BlockSpec((tm, d), row), pl.BlockSpec((tm, d), row)],
        out_shape=[jax.ShapeDtypeStruct((t, d), F32), jax.ShapeDtypeStruct((t, d), BF16)],
        compiler_params=_cparams("parallel"),
        name="mix",
    )(z, on, sg, x, wc, wh, wo, gt, g2.reshape(1, d), sc, sh)


def _ffn_kernel(h_ref, wg_ref, wu_ref, wd_ref, x1_ref, gt_ref, gf_ref, o_ref, acc_ref):
    j = pl.program_id(1)

    @pl.when(j == 0)
    def _():
        acc_ref[...] = jnp.zeros_like(acc_ref)

    h = h_ref[...]
    act = (_silu(_dot(h, wg_ref[...])) * _dot(h, wu_ref[...])).astype(BF16)
    acc_ref[...] += _dot(act, wd_ref[...])

    @pl.when(j == pl.num_programs(1) - 1)
    def _():
        x2 = x1_ref[...] + gt_ref[...] * acc_ref[...]
        o_ref[...] = _rms(x2) * gf_ref[...]


def _ffn(h2, wg, wu, wd, x1, gt, gf, seq):
    t, d = x1.shape
    dff = wg.shape[1]
    tm, tf = 512, 512
    per = seq // tm
    return pl.pallas_call(
        _ffn_kernel,
        grid=(t // tm, dff // tf),
        in_specs=[pl.BlockSpec((tm, d), lambda i, j: (i, 0)),
                  pl.BlockSpec((d, tf), lambda i, j: (0, j)),
                  pl.BlockSpec((d, tf), lambda i, j: (0, j)),
                  pl.BlockSpec((tf, d), lambda i, j: (j, 0)),
                  pl.BlockSpec((tm, d), lambda i, j: (i, 0)),
                  pl.BlockSpec((None, 1, d), lambda i, j: (i // per, 0, 0)),
                  pl.BlockSpec((1, d), lambda i, j: (0, 0))],
        out_specs=pl.BlockSpec((tm, d), lambda i, j: (i, 0)),
        out_shape=jax.ShapeDtypeStruct((t, d), F32),
        scratch_shapes=[pltpu.VMEM((tm, d), F32)],
        compiler_params=_cparams("parallel", "arbitrary"),
        name="ffn",
    )(h2, wg, wu, wd, x1, gt, gf.reshape(1, d))


def kernel(x, c, w_ada, b_ada, norm_mix_g, w_in, conv_w, lb_param, gnorm_g, w_conv_out,
           w_hgrn_out, w_o, norm_ffn_g, w_ffn_gate, w_ffn_up, w_ffn_down, norm_final_g):
    b, s, d = x.shape
    t = b * s
    depth = w_in.shape[0]
    dc = conv_w.shape[2]
    dk = lb_param.shape[1]
    dv = w_hgrn_out.shape[1]
    assert dk == HEADS * HEAD_DIM and dv == HEADS * HEAD_DIM and s % CHUNK == 0
    tm = 1024
    cw = 256
    assert s % tm == 0 and dc % cw == 0

    assert depth == 1
    l = 0

    mod = _modulation(c, w_ada[l], b_ada[l]).reshape(b, 6, 1, d)
    sh_m, sc_m, gt_m, sh_f, sc_f, gt_f = [mod[:, i] for i in range(6)]

    h = _hnorm(x, norm_mix_g[l], sc_m, sh_m).reshape(t, d)

    w = w_in[l]
    o = 0
    w_b, w_c, w_x = [w[:, o + i * dc:o + (i + 1) * dc] for i in range(3)]
    o += 3 * dc
    w_q, w_f = w[:, o:o + dk], w[:, o + dk:o + 2 * dk]
    o += 2 * dk
    w_i, w_g = w[:, o:o + dv], w[:, o + dv:o + 2 * dv]
    o += 2 * dv
    w_gate = w[:, o:]
    w_conv = jnp.concatenate(
        [wp[:, j * cw:(j + 1) * cw] for j in range(dc // cw) for wp in (w_b, w_c, w_x)], axis=1)

    z, = _proj_call(
        functools.partial(_conv_body, tm=tm, cw=cw, tiles_per_seq=s // tm),
        h, w_conv.astype(BF16), [conv_w[l]], [pl.BlockSpec((CONV_WIDTH, cw), lambda j, i: (0, j))],
        [cw], [BF16], tm, 3 * cw, "proj_conv",
        scratch_shapes=[pltpu.VMEM((tm + SUBLANES, cw), F32)])
    q, = _proj_call(_silu_body, h, w_q.astype(BF16), [], [], [dk], [BF16], tm, dk, "proj_q")
    kk, lf = _proj_call(
        functools.partial(_forget_body, layer=l), h, w_f.astype(BF16), [lb_param],
        [pl.BlockSpec((lb_param.shape[0], dk), lambda j, i: (0, 0))],
        [dk, dk], [BF16, F32], tm, dk, "proj_f")
    v, = _proj_call(_ident_body, h, w_i.astype(BF16), [], [], [dv], [BF16], tm, dv, "proj_i")
    gs, = _proj_call(_silu_body, h, w_g.astype(BF16), [], [], [dv], [BF16], tm, dv, "proj_g")
    sg, = _proj_call(_sigmoid_body, h, w_gate.astype(BF16), [], [], [1024], [BF16], tm, 1024,
                     "proj_gate")

    on = _hgrn(q, kk, lf, v, gs, gnorm_g[l], b, s)

    x1, h2 = _mix(z, on, sg, x.reshape(t, d), w_conv_out[l].astype(BF16),
                  w_hgrn_out[l].astype(BF16), w_o[l].astype(BF16), gt_m, norm_ffn_g[l],
                  sc_f, sh_f, s)
    out = _ffn(h2, w_ffn_gate[l].astype(BF16), w_ffn_up[l].astype(BF16),
               w_ffn_down[l].astype(BF16), x1, gt_f, norm_final_g, s)
    return out.reshape(b, s, d)
```

```python
import functools

import jax
import jax.numpy as jnp
import numpy as np
from jax import lax
from jax.experimental import pallas as pl
from jax.experimental.pallas import tpu as pltpu

F32 = jnp.float32
BF16 = jnp.bfloat16

EPS = 1e-6
CONV_WIDTH = 3
HEADS = 8
HEAD_DIM = 128
CHUNK = 128
LEVELS = 8
FINE = 3
SUBLANES = 8
VMEM_LIMIT = 56 * 1024 * 1024


def _cparams(*sem):
    return pltpu.CompilerParams(dimension_semantics=sem, vmem_limit_bytes=VMEM_LIMIT)


def _dot(a, b):
    return jnp.dot(a, b, preferred_element_type=F32)


def _dot_nt(a, b):
    return lax.dot_general(a, b, (((1,), (1,)), ((), ())), preferred_element_type=F32)


def _silu(x):
    return x * jax.nn.sigmoid(x)


def _rms(x):
    return x * lax.rsqrt(jnp.mean(x * x, axis=-1, keepdims=True) + EPS)


def _mod_kernel(c_ref, w_ref, b_ref, o_ref):
    c = c_ref[...]
    o_ref[...] = jnp.dot(_silu(c), w_ref[...], preferred_element_type=F32,
                         precision=lax.Precision.HIGHEST) + b_ref[...]


def _modulation(c, w_ada, b_ada):
    b, d = c.shape
    n = w_ada.shape[1]
    tn = 1024
    return pl.pallas_call(
        _mod_kernel,
        grid=(n // tn,),
        in_specs=[pl.BlockSpec((b, d), lambda j: (0, 0)),
                  pl.BlockSpec((d, tn), lambda j: (0, j)),
                  pl.BlockSpec((1, tn), lambda j: (0, j))],
        out_specs=pl.BlockSpec((b, tn), lambda j: (0, j)),
        out_shape=jax.ShapeDtypeStruct((b, n), F32),
        compiler_params=_cparams("parallel"),
        name="mod",
    )(c, w_ada, b_ada.reshape(1, n))


def _hnorm_kernel(x_ref, g_ref, sc_ref, sh_ref, o_ref):
    y = _rms(x_ref[...]) * g_ref[...]
    o_ref[...] = (y * (1.0 + sc_ref[...]) + sh_ref[...]).astype(BF16)


def _hnorm(x, g, sc, sh):
    b, s, d = x.shape
    tr = 512
    vec = pl.BlockSpec((None, 1, d), lambda bi, r: (bi, 0, 0))
    return pl.pallas_call(
        _hnorm_kernel,
        grid=(b, s // tr),
        in_specs=[pl.BlockSpec((None, tr, d), lambda bi, r: (bi, r, 0)),
                  pl.BlockSpec((1, d), lambda bi, r: (0, 0)), vec, vec],
        out_specs=pl.BlockSpec((None, tr, d), lambda bi, r: (bi, r, 0)),
        out_shape=jax.ShapeDtypeStruct((b, s, d), BF16),
        compiler_params=_cparams("parallel", "parallel"),
        name="hnorm",
    )(x, g.reshape(1, d), sc, sh)


def _proj_call(body, h, w, extras, extra_specs, out_widths, out_dtypes, tm, tn, name,
               scratch_shapes=()):
    t, d = h.shape
    n = w.shape[1]
    nj = n // tn
    in_specs = [pl.BlockSpec((tm, d), lambda j, i: (i, 0)),
                pl.BlockSpec((d, tn), lambda j, i: (0, j))] + list(extra_specs)
    out_specs = [pl.BlockSpec((tm, ow), lambda j, i: (i, j)) for ow in out_widths]
    out_shape = [jax.ShapeDtypeStruct((t, nj * ow), dt) for ow, dt in zip(out_widths, out_dtypes)]
    return pl.pallas_call(
        body,
        grid=(nj, t // tm),
        in_specs=in_specs,
        out_specs=out_specs,
        out_shape=out_shape,
        scratch_shapes=list(scratch_shapes),
        compiler_params=_cparams("arbitrary", "arbitrary"),
        name=name,
    )(h, w, *extras)


def _conv_body(h_ref, w_ref, cw_ref, o_ref, ubuf, *, tm, cw, tiles_per_seq):
    i = pl.program_id(1)
    acc = _dot(h_ref[...], w_ref[...])
    u = acc[:, cw:2 * cw] * acc[:, 2 * cw:]

    @pl.when(i % tiles_per_seq == 0)
    def _():
        ubuf[0:SUBLANES, :] = jnp.zeros((SUBLANES, cw), F32)

    ubuf[SUBLANES:SUBLANES + tm, :] = u
    wk = cw_ref[...]
    y = wk[2:3, :] * u
    y = y + wk[1:2, :] * ubuf[SUBLANES - 1:SUBLANES - 1 + tm, :]
    y = y + wk[0:1, :] * ubuf[SUBLANES - 2:SUBLANES - 2 + tm, :]
    o_ref[...] = (acc[:, :cw] * y).astype(BF16)
    ubuf[0:SUBLANES, :] = ubuf[tm:tm + SUBLANES, :]


def _silu_body(h_ref, w_ref, o_ref):
    o_ref[...] = _silu(_dot(h_ref[...], w_ref[...])).astype(BF16)


def _gate_gain_body(h_ref, w_ref, gain_ref, o_ref):
    o_ref[...] = (_silu(_dot(h_ref[...], w_ref[...])) * gain_ref[...]).astype(BF16)


def _ident_body(h_ref, w_ref, o_ref):
    o_ref[...] = _dot(h_ref[...], w_ref[...]).astype(BF16)


def _sigmoid_body(h_ref, w_ref, o_ref):
    o_ref[...] = jax.nn.sigmoid(_dot(h_ref[...], w_ref[...])).astype(BF16)


def _forget_body(h_ref, w_ref, lbp_ref, k_ref, lf_ref, *, layer):
    p = lbp_ref[...]
    e = jnp.exp(p - jnp.max(p, axis=0, keepdims=True))
    sm = e / jnp.sum(e, axis=0, keepdims=True)
    lb = jnp.sum(sm[:layer + 1, :], axis=0, keepdims=True)
    f = lb + (1.0 - lb) * jax.nn.sigmoid(_dot(h_ref[...], w_ref[...]))
    k_ref[...] = (1.0 - f).astype(BF16)
    lf_ref[...] = jnp.log2(f)


def _level_tables():
    n = CHUNK
    level = -np.ones((n, n), np.int32)
    expo = np.zeros((FINE + 1, n, n), np.float32)
    for t in range(n):
        level[t, t] = 0
        for s in range(t):
            level[t, s] = (t ^ s).bit_length()
        for li in range(1, FINE + 1):
            m = 1 << (li - 1)
            blk = (t // m) * m
            if (t % (2 * m)) >= m:
                expo[li - 1, t, blk:t + 1] = 1.0
            else:
                expo[li - 1, t, t + 1:blk + m] = 1.0
        expo[FINE, t, :t + 1] = 1.0
    expo = expo.reshape((FINE + 1) * n, n)
    return level, np.concatenate([expo, expo], axis=1)


def _hgrn_kernel(q_ref, k_ref, lf_ref, v_ref, gs_ref, lv_ref, ex_ref, o_ref, st_ref, dec_ref):
    c = pl.program_id(1)

    @pl.when(c == 0)
    def _():
        st_ref[...] = jnp.zeros_like(st_ref)

    n = CHUNK
    nb = n // SUBLANES
    g0 = FINE * n
    tile = lambda a, r: a[r * SUBLANES:(r + 1) * SUBLANES, :]

    lf = lf_ref[...]
    hi = lf.astype(BF16)
    lo = (lf - hi.astype(F32)).astype(BF16)
    dec_ref[...] = _dot(ex_ref[...], jnp.concatenate([hi, lo], axis=0))

    for hd in range(HEADS):
        sl = slice(hd * HEAD_DIM, (hd + 1) * HEAD_DIM)
        qb = q_ref[:, sl]
        kb = k_ref[:, sl]
        vb = v_ref[:, sl]
        q = qb.astype(F32)
        k = kb.astype(F32)
        g = dec_ref[g0:g0 + n, sl]

        def halves_exponent(m):
            parts = []
            for r0 in range(0, n, 2 * m):
                g_n = dec_ref[g0 + r0 + m - 1:g0 + r0 + m, sl]
                parts += [g_n - g[r0:r0 + m], g[r0 + m:r0 + 2 * m] - g_n]
            return jnp.concatenate(parts, axis=0)

        p = _dot_nt(qb, kb)
        sc = [jnp.where(tile(lv_ref, r) == 0, tile(p, r), 0.0) for r in range(nb)]
        for li in range(1, LEVELS):
            m = 1 << (li - 1)
            x = dec_ref[(li - 1) * n:li * n, sl] if li <= FINE else halves_exponent(m)
            e = jnp.exp2(x)
            if m < SUBLANES:
                rows = list(range(nb))
                p = _dot_nt((q * e).astype(BF16), (k * e).astype(BF16))
            else:
                q_parts, k_parts, rows = [], [], []
                for r0 in range(0, n, 2 * m):
                    k_parts += [k[r0:r0 + m] * e[r0:r0 + m], jnp.zeros((m, HEAD_DIM), F32)]
                    q_parts += [q[r0 + m:r0 + 2 * m] * e[r0 + m:r0 + 2 * m]]
                    rows += list(range((r0 + m) // SUBLANES, (r0 + 2 * m) // SUBLANES))
                p = _dot_nt(jnp.concatenate(q_parts, axis=0).astype(BF16),
                            jnp.concatenate(k_parts, axis=0).astype(BF16))
            for i, r in enumerate(rows):
                sc[r] = jnp.where(tile(lv_ref, r) == li, tile(p, i), sc[r])
        scores = jnp.concatenate(sc, axis=0).astype(BF16)
        g_last = dec_ref[g0 + n - 1:g0 + n, sl]
        q_in = (q * jnp.exp2(g)).astype(BF16)
        k_out = (k * jnp.exp2(g_last - g)).astype(BF16)
        st = st_ref[hd]
        o = _dot_nt(q_in, st.astype(BF16)) + _dot(scores, vb)
        vt = vb.astype(F32).T.astype(BF16)
        st_ref[hd] = st * jnp.exp2(g_last) + _dot(vt, k_out)
        o_ref[:, sl] = (_rms(o) * gs_ref[:, sl].astype(F32)).astype(BF16)


def _hgrn(q, k, lf, v, gs, batch, seq):
    t, dk = q.shape
    nc = seq // CHUNK
    level, expo = _level_tables()
    row = lambda b, c: (b * nc + c, 0)
    const = lambda b, c: (0, 0)
    blk = pl.BlockSpec((CHUNK, dk), row)
    return pl.pallas_call(
        _hgrn_kernel,
        grid=(batch, nc),
        in_specs=[blk, blk, blk, blk, blk,
                  pl.BlockSpec((CHUNK, CHUNK), const),
                  pl.BlockSpec(expo.shape, const)],
        out_specs=blk,
        out_shape=jax.ShapeDtypeStruct((t, dk), BF16),
        scratch_shapes=[pltpu.VMEM((HEADS, HEAD_DIM, HEAD_DIM), F32),
                        pltpu.VMEM(((FINE + 1) * CHUNK, dk), F32)],
        compiler_params=_cparams("arbitrary", "arbitrary"),
        name="hgrn",
    )(q, k, lf, v, gs, jnp.asarray(level), jnp.asarray(expo, BF16))


def _mix_kernel(z_ref, on_ref, sg_ref, x_ref, wc_ref, wh_ref, wo_ref, gt_ref, g2_ref, sc_ref,
                sh_ref, x1_ref, h2_ref, *, d):
    y_a = _dot(z_ref[...], wc_ref[...])
    y_b = _dot(on_ref[...], wh_ref[...])
    merged = sg_ref[:, :d].astype(F32) * y_a + sg_ref[:, d:].astype(F32) * y_b
    x1 = x_ref[...] + gt_ref[...] * _dot(merged.astype(BF16), wo_ref[...])
    x1_ref[...] = x1
    h2_ref[...] = (_rms(x1) * g2_ref[...] * (1.0 + sc_ref[...]) + sh_ref[...]).astype(BF16)


def _mix(z, on, sg, x, wc, wh, wo, gt, g2, sc, sh, seq):
    t, d = x.shape
    tm = 256
    per = seq // tm
    row = lambda i: (i, 0)
    const = lambda i: (0, 0)
    vec = pl.BlockSpec((None, 1, d), lambda i: (i // per, 0, 0))
    resident = lambda shape: pl.BlockSpec(shape, const, pipeline_mode=pl.Buffered(1))
    return pl.pallas_call(
        functools.partial(_mix_kernel, d=d),
        grid=(t // tm,),
        in_specs=[pl.BlockSpec((tm, z.shape[1]), row),
                  pl.BlockSpec((tm, on.shape[1]), row),
                  pl.BlockSpec((tm, 2 * d), row),
                  pl.BlockSpec((tm, d), row),
                  resident(wc.shape), resident(wh.shape), resident(wo.shape),
                  vec, pl.BlockSpec((1, d), const), vec, vec],
        out_specs=[pl.BlockSpec((tm, d), row), pl.BlockSpec((tm, d), row)],
        out_shape=[jax.ShapeDtypeStruct((t, d), F32), jax.ShapeDtypeStruct((t, d), BF16)],
        compiler_params=_cparams("parallel"),
        name="mix",
    )(z, on, sg, x, wc, wh, wo, gt, g2.reshape(1, d), sc, sh)


def _ffn_kernel(h_ref, wg_ref, wu_ref, wd_ref, x1_ref, gt_ref, gf_ref, o_ref, acc_ref):
    j = pl.program_id(1)

    @pl.when(j == 0)
    def _():
        acc_ref[...] = jnp.zeros_like(acc_ref)

    h = h_ref[...]
    act = (_silu(_dot(h, wg_ref[...])) * _dot(h, wu_ref[...])).astype(BF16)
    acc_ref[...] += _dot(act, wd_ref[...])

    @pl.when(j == pl.num_programs(1) - 1)
    def _():
        x2 = x1_ref[...] + gt_ref[...] * acc_ref[...]
        o_ref[...] = _rms(x2) * gf_ref[...]


def _ffn(h2, wg, wu, wd, x1, gt, gf, seq):
    t, d = x1.shape
    dff = wg.shape[1]
    tm, tf = 512, 512
    per = seq // tm
    return pl.pallas_call(
        _ffn_kernel,
        grid=(t // tm, dff // tf),
        in_specs=[pl.BlockSpec((tm, d), lambda i, j: (i, 0)),
                  pl.BlockSpec((d, tf), lambda i, j: (0, j)),
                  pl.BlockSpec((d, tf), lambda i, j: (0, j)),
                  pl.BlockSpec((tf, d), lambda i, j: (j, 0)),
                  pl.BlockSpec((tm, d), lambda i, j: (i, 0)),
                  pl.BlockSpec((None, 1, d), lambda i, j: (i // per, 0, 0)),
                  pl.BlockSpec((1, d), lambda i, j: (0, 0))],
        out_specs=pl.BlockSpec((tm, d), lambda i, j: (i, 0)),
        out_shape=jax.ShapeDtypeStruct((t, d), F32),
        scratch_shapes=[pltpu.VMEM((tm, d), F32)],
        compiler_params=_cparams("parallel", "arbitrary"),
        name="ffn",
    )(h2, wg, wu, wd, x1, gt, gf.reshape(1, d))


def kernel(x, c, w_ada, b_ada, norm_mix_g, w_in, conv_w, lb_param, gnorm_g, w_conv_out,
           w_hgrn_out, w_o, norm_ffn_g, w_ffn_gate, w_ffn_up, w_ffn_down, norm_final_g):
    b, s, d = x.shape
    t = b * s
    depth = w_in.shape[0]
    dc = conv_w.shape[2]
    dk = lb_param.shape[1]
    dv = w_hgrn_out.shape[1]
    assert dk == HEADS * HEAD_DIM and dv == HEADS * HEAD_DIM and s % CHUNK == 0
    tm = 1024
    cw = 256
    assert s % tm == 0 and dc % cw == 0

    assert depth == 1
    l = 0

    mod = _modulation(c, w_ada[l], b_ada[l]).reshape(b, 6, 1, d)
    sh_m, sc_m, gt_m, sh_f, sc_f, gt_f = [mod[:, i] for i in range(6)]

    h = _hnorm(x, norm_mix_g[l], sc_m, sh_m).reshape(t, d)

    w = w_in[l]
    o = 0
    w_b, w_c, w_x = [w[:, o + i * dc:o + (i + 1) * dc] for i in range(3)]
    o += 3 * dc
    w_q, w_f = w[:, o:o + dk], w[:, o + dk:o + 2 * dk]
    o += 2 * dk
    w_i, w_g = w[:, o:o + dv], w[:, o + dv:o + 2 * dv]
    o += 2 * dv
    w_gate = w[:, o:]
    w_conv = jnp.concatenate(
        [wp[:, j * cw:(j + 1) * cw] for j in range(dc // cw) for wp in (w_b, w_c, w_x)], axis=1)

    z, = _proj_call(
        functools.partial(_conv_body, tm=tm, cw=cw, tiles_per_seq=s // tm),
        h, w_conv.astype(BF16), [conv_w[l]], [pl.BlockSpec((CONV_WIDTH, cw), lambda j, i: (0, j))],
        [cw], [BF16], tm, 3 * cw, "proj_conv",
        scratch_shapes=[pltpu.VMEM((tm + SUBLANES, cw), F32)])
    q, = _proj_call(_silu_body, h, w_q.astype(BF16), [], [], [dk], [BF16], tm, dk, "proj_q")
    kk, lf = _proj_call(
        functools.partial(_forget_body, layer=l), h, w_f.astype(BF16), [lb_param],
        [pl.BlockSpec((lb_param.shape[0], dk), lambda j, i: (0, 0))],
        [dk, dk], [BF16, F32], tm, dk, "proj_f")
    v, = _proj_call(_ident_body, h, w_i.astype(BF16), [], [], [dv], [BF16], tm, dv, "proj_i")
    gs, = _proj_call(_gate_gain_body, h, w_g.astype(BF16), [jnp.tile(gnorm_g[l], HEADS).reshape(1, dv)],
                     [pl.BlockSpec((1, dv), lambda j, i: (0, 0))], [dv], [BF16], tm, dv, "proj_g")
    sg, = _proj_call(_sigmoid_body, h, w_gate.astype(BF16), [], [], [1024], [BF16], tm, 1024,
                     "proj_gate")

    on = _hgrn(q, kk, lf, v, gs, b, s)

    x1, h2 = _mix(z, on, sg, x.reshape(t, d), w_conv_out[l].astype(BF16),
                  w_hgrn_out[l].astype(BF16), w_o[l].astype(BF16), gt_m, norm_ffn_g[l],
                  sc_f, sh_f, s)
    out = _ffn(h2, w_ffn_gate[l].astype(BF16), w_ffn_up[l].astype(BF16),
               w_ffn_down[l].astype(BF16), x1, gt_f, norm_final_g, s)
    return out.reshape(b, s, d)
```

```python
import functools

import jax
import jax.numpy as jnp
import numpy as np
from jax import lax
from jax.experimental import pallas as pl
from jax.experimental.pallas import tpu as pltpu

F32 = jnp.float32
BF16 = jnp.bfloat16

EPS = 1e-6
CONV_WIDTH = 3
HEADS = 8
HEAD_DIM = 128
CHUNK = 128
LEVELS = 8
FINE = 3
SUBLANES = 8
VMEM_LIMIT = 56 * 1024 * 1024


def _cparams(*sem):
    return pltpu.CompilerParams(dimension_semantics=sem, vmem_limit_bytes=VMEM_LIMIT)


def _dot(a, b):
    return jnp.dot(a, b, preferred_element_type=F32)


def _dot_nt(a, b):
    return lax.dot_general(a, b, (((1,), (1,)), ((), ())), preferred_element_type=F32)


def _silu(x):
    return x * jax.nn.sigmoid(x)


def _rms(x):
    return x * lax.rsqrt(jnp.mean(x * x, axis=-1, keepdims=True) + EPS)


def _mod_kernel(c_ref, w_ref, b_ref, o_ref):
    c = c_ref[...]
    o_ref[...] = jnp.dot(_silu(c), w_ref[...], preferred_element_type=F32,
                         precision=lax.Precision.HIGHEST) + b_ref[...]


def _modulation(c, w_ada, b_ada):
    b, d = c.shape
    n = w_ada.shape[1]
    tn = 1024
    return pl.pallas_call(
        _mod_kernel,
        grid=(n // tn,),
        in_specs=[pl.BlockSpec((b, d), lambda j: (0, 0)),
                  pl.BlockSpec((d, tn), lambda j: (0, j)),
                  pl.BlockSpec((1, tn), lambda j: (0, j))],
        out_specs=pl.BlockSpec((b, tn), lambda j: (0, j)),
        out_shape=jax.ShapeDtypeStruct((b, n), F32),
        compiler_params=_cparams("parallel"),
        name="mod",
    )(c, w_ada, b_ada.reshape(1, n))


def _hnorm_kernel(x_ref, g_ref, sc_ref, sh_ref, o_ref):
    y = _rms(x_ref[...]) * g_ref[...]
    o_ref[...] = (y * (1.0 + sc_ref[...]) + sh_ref[...]).astype(BF16)


def _hnorm(x, g, sc, sh):
    b, s, d = x.shape
    tr = 512
    vec = pl.BlockSpec((None, 1, d), lambda bi, r: (bi, 0, 0))
    return pl.pallas_call(
        _hnorm_kernel,
        grid=(b, s // tr),
        in_specs=[pl.BlockSpec((None, tr, d), lambda bi, r: (bi, r, 0)),
                  pl.BlockSpec((1, d), lambda bi, r: (0, 0)), vec, vec],
        out_specs=pl.BlockSpec((None, tr, d), lambda bi, r: (bi, r, 0)),
        out_shape=jax.ShapeDtypeStruct((b, s, d), BF16),
        compiler_params=_cparams("parallel", "parallel"),
        name="hnorm",
    )(x, g.reshape(1, d), sc, sh)


def _proj_kernel(*refs, n_w, n_extra, n_out, epilogue):
    h_ref = refs[0]
    w_refs = refs[1:1 + n_w]
    extra = refs[1 + n_w:1 + n_w + n_extra]
    outs = refs[1 + n_w + n_extra:1 + n_w + n_extra + n_out]
    wbf = refs[1 + n_w + n_extra + n_out]
    scratch = refs[2 + n_w + n_extra + n_out:]

    @pl.when(pl.program_id(1) == 0)
    def _():
        off = 0
        for w_ref in w_refs:
            wd = w_ref.shape[1]
            wbf[:, off:off + wd] = w_ref[...].astype(BF16)
            off += wd

    epilogue(_dot(h_ref[...], wbf[...]), extra, outs, scratch)


def _proj_call(epilogue, h, w, col_starts, wd, nj, extras, extra_specs, out_widths, out_dtypes,
               tm, name, scratch_shapes=()):
    t, d = h.shape
    in_specs = [pl.BlockSpec((tm, d), lambda j, i: (i, 0))]
    for c0 in col_starts:
        in_specs.append(pl.BlockSpec((d, wd), functools.partial(lambda j, i, b: (0, b + j), b=c0 // wd)))
    in_specs += list(extra_specs)
    out_specs = [pl.BlockSpec((tm, ow), lambda j, i: (i, j)) for ow in out_widths]
    out_shape = [jax.ShapeDtypeStruct((t, nj * ow), dt) for ow, dt in zip(out_widths, out_dtypes)]
    kern = functools.partial(_proj_kernel, n_w=len(col_starts), n_extra=len(extras),
                             n_out=len(out_widths), epilogue=epilogue)
    return pl.pallas_call(
        kern,
        grid=(nj, t // tm),
        in_specs=in_specs,
        out_specs=out_specs,
        out_shape=out_shape,
        scratch_shapes=[pltpu.VMEM((d, wd * len(col_starts)), BF16)] + list(scratch_shapes),
        compiler_params=_cparams("arbitrary", "arbitrary"),
        name=name,
    )(h, *([w] * len(col_starts)), *extras)


def _conv_epilogue(acc, extra, outs, scratch, *, tm, cw, tiles_per_seq):
    cw_ref, = extra
    o_ref, = outs
    ubuf, = scratch
    i = pl.program_id(1)
    u = acc[:, cw:2 * cw] * acc[:, 2 * cw:]

    @pl.when(i % tiles_per_seq == 0)
    def _():
        ubuf[0:SUBLANES, :] = jnp.zeros((SUBLANES, cw), F32)

    ubuf[SUBLANES:SUBLANES + tm, :] = u
    wk = cw_ref[...]
    y = wk[2:3, :] * u
    y = y + wk[1:2, :] * ubuf[SUBLANES - 1:SUBLANES - 1 + tm, :]
    y = y + wk[0:1, :] * ubuf[SUBLANES - 2:SUBLANES - 2 + tm, :]
    o_ref[...] = (acc[:, :cw] * y).astype(BF16)
    ubuf[0:SUBLANES, :] = ubuf[tm:tm + SUBLANES, :]


def _silu_epilogue(acc, extra, outs, scratch):
    outs[0][...] = _silu(acc).astype(BF16)


def _gate_gain_epilogue(acc, extra, outs, scratch):
    outs[0][...] = (_silu(acc) * extra[0][...]).astype(BF16)


def _ident_epilogue(acc, extra, outs, scratch):
    outs[0][...] = acc.astype(BF16)


def _sigmoid_epilogue(acc, extra, outs, scratch):
    outs[0][...] = jax.nn.sigmoid(acc).astype(BF16)


def _forget_epilogue(acc, extra, outs, scratch, *, layer):
    k_ref, lf_ref = outs
    p = extra[0][...]
    e = jnp.exp(p - jnp.max(p, axis=0, keepdims=True))
    sm = e / jnp.sum(e, axis=0, keepdims=True)
    lb = jnp.sum(sm[:layer + 1, :], axis=0, keepdims=True)
    f = lb + (1.0 - lb) * jax.nn.sigmoid(acc)
    k_ref[...] = (1.0 - f).astype(BF16)
    lf_ref[...] = jnp.log2(f)


def _level_tables():
    n = CHUNK
    level = -np.ones((n, n), np.int32)
    expo = np.zeros((FINE + 1, n, n), np.float32)
    for t in range(n):
        level[t, t] = 0
        for s in range(t):
            level[t, s] = (t ^ s).bit_length()
        for li in range(1, FINE + 1):
            m = 1 << (li - 1)
            blk = (t // m) * m
            if (t % (2 * m)) >= m:
                expo[li - 1, t, blk:t + 1] = 1.0
            else:
                expo[li - 1, t, t + 1:blk + m] = 1.0
        expo[FINE, t, :t + 1] = 1.0
    expo = expo.reshape((FINE + 1) * n, n)
    return level, np.concatenate([expo, expo], axis=1)


def _hgrn_kernel(q_ref, k_ref, lf_ref, v_ref, gs_ref, lv_ref, ex_ref, o_ref, st_ref, dec_ref):
    c = pl.program_id(1)

    @pl.when(c == 0)
    def _():
        st_ref[...] = jnp.zeros_like(st_ref)

    n = CHUNK
    nb = n // SUBLANES
    g0 = FINE * n
    tile = lambda a, r: a[r * SUBLANES:(r + 1) * SUBLANES, :]

    lf = lf_ref[...]
    hi = lf.astype(BF16)
    lo = (lf - hi.astype(F32)).astype(BF16)
    dec_ref[...] = _dot(ex_ref[...], jnp.concatenate([hi, lo], axis=0))

    for hd in range(HEADS):
        sl = slice(hd * HEAD_DIM, (hd + 1) * HEAD_DIM)
        qb = q_ref[:, sl]
        kb = k_ref[:, sl]
        vb = v_ref[:, sl]
        q = qb.astype(F32)
        k = kb.astype(F32)
        g = dec_ref[g0:g0 + n, sl]

        def halves_exponent(m):
            parts = []
            for r0 in range(0, n, 2 * m):
                g_n = dec_ref[g0 + r0 + m - 1:g0 + r0 + m, sl]
                parts += [g_n - g[r0:r0 + m], g[r0 + m:r0 + 2 * m] - g_n]
            return jnp.concatenate(parts, axis=0)

        p = _dot_nt(qb, kb)
        sc = [jnp.where(tile(lv_ref, r) == 0, tile(p, r), 0.0) for r in range(nb)]
        for li in range(1, LEVELS):
            m = 1 << (li - 1)
            x = dec_ref[(li - 1) * n:li * n, sl] if li <= FINE else halves_exponent(m)
            e = jnp.exp2(x)
            if m < SUBLANES:
                rows = list(range(nb))
                p = _dot_nt((q * e).astype(BF16), (k * e).astype(BF16))
            else:
                q_parts, k_parts, rows = [], [], []
                for r0 in range(0, n, 2 * m):
                    k_parts += [k[r0:r0 + m] * e[r0:r0 + m], jnp.zeros((m, HEAD_DIM), F32)]
                    q_parts += [q[r0 + m:r0 + 2 * m] * e[r0 + m:r0 + 2 * m]]
                    rows += list(range((r0 + m) // SUBLANES, (r0 + 2 * m) // SUBLANES))
                p = _dot_nt(jnp.concatenate(q_parts, axis=0).astype(BF16),
                            jnp.concatenate(k_parts, axis=0).astype(BF16))
            for i, r in enumerate(rows):
                sc[r] = jnp.where(tile(lv_ref, r) == li, tile(p, i), sc[r])
        scores = jnp.concatenate(sc, axis=0).astype(BF16)
        g_last = dec_ref[g0 + n - 1:g0 + n, sl]
        q_in = (q * jnp.exp2(g)).astype(BF16)
        k_out = (k * jnp.exp2(g_last - g)).astype(BF16)
        st = st_ref[hd]
        o = _dot_nt(q_in, st.astype(BF16)) + _dot(scores, vb)
        vt = vb.astype(F32).T.astype(BF16)
        st_ref[hd] = st * jnp.exp2(g_last) + _dot(vt, k_out)
        o_ref[:, sl] = (_rms(o) * gs_ref[:, sl].astype(F32)).astype(BF16)


def _hgrn(q, k, lf, v, gs, batch, seq):
    t, dk = q.shape
    nc = seq // CHUNK
    level, expo = _level_tables()
    row = lambda b, c: (b * nc + c, 0)
    const = lambda b, c: (0, 0)
    blk = pl.BlockSpec((CHUNK, dk), row)
    return pl.pallas_call(
        _hgrn_kernel,
        grid=(batch, nc),
        in_specs=[blk, blk, blk, blk, blk,
                  pl.BlockSpec((CHUNK, CHUNK), const),
                  pl.BlockSpec(expo.shape, const)],
        out_specs=blk,
        out_shape=jax.ShapeDtypeStruct((t, dk), BF16),
        scratch_shapes=[pltpu.VMEM((HEADS, HEAD_DIM, HEAD_DIM), F32),
                        pltpu.VMEM(((FINE + 1) * CHUNK, dk), F32)],
        compiler_params=_cparams("arbitrary", "arbitrary"),
        name="hgrn",
    )(q, k, lf, v, gs, jnp.asarray(level), jnp.asarray(expo, BF16))


def _mix_kernel(z_ref, on_ref, sg_ref, x_ref, wc_ref, wh_ref, wo_ref, gt_ref, g2_ref, sc_ref,
                sh_ref, x1_ref, h2_ref, *, d):
    y_a = _dot(z_ref[...], wc_ref[...])
    y_b = _dot(on_ref[...], wh_ref[...])
    merged = sg_ref[:, :d].astype(F32) * y_a + sg_ref[:, d:].astype(F32) * y_b
    x1 = x_ref[...] + gt_ref[...] * _dot(merged.astype(BF16), wo_ref[...])
    x1_ref[...] = x1
    h2_ref[...] = (_rms(x1) * g2_ref[...] * (1.0 + sc_ref[...]) + sh_ref[...]).astype(BF16)


def _mix(z, on, sg, x, wc, wh, wo, gt, g2, sc, sh, seq):
    t, d = x.shape
    tm = 256
    per = seq // tm
    row = lambda i: (i, 0)
    const = lambda i: (0, 0)
    vec = pl.BlockSpec((None, 1, d), lambda i: (i // per, 0, 0))
    resident = lambda shape: pl.BlockSpec(shape, const, pipeline_mode=pl.Buffered(1))
    return pl.pallas_call(
        functools.partial(_mix_kernel, d=d),
        grid=(t // tm,),
        in_specs=[pl.BlockSpec((tm, z.shape[1]), row),
                  pl.BlockSpec((tm, on.shape[1]), row),
                  pl.BlockSpec((tm, 2 * d), row),
                  pl.BlockSpec((tm, d), row),
                  resident(wc.shape), resident(wh.shape), resident(wo.shape),
                  vec, pl.BlockSpec((1, d), const), vec, vec],
        out_specs=[pl.BlockSpec((tm, d), row), pl.BlockSpec((tm, d), row)],
        out_shape=[jax.ShapeDtypeStruct((t, d), F32), jax.ShapeDtypeStruct((t, d), BF16)],
        compiler_params=_cparams("parallel"),
        name="mix",
    )(z, on, sg, x, wc, wh, wo, gt, g2.reshape(1, d), sc, sh)


def _ffn_kernel(h_ref, wg_ref, wu_ref, wd_ref, x1_hbm, gt_ref, gf_ref, o_ref, x1_buf, x1_sem, *, tm):
    i = pl.program_id(0)
    j = pl.program_id(1)

    def x1_copy():
        return pltpu.make_async_copy(x1_hbm.at[pl.ds(i * tm, tm), :], x1_buf, x1_sem)

    @pl.when(j == 0)
    def _():
        x1_copy().start()
        o_ref[...] = jnp.zeros_like(o_ref)

    half = tm // 2
    for r0 in range(0, tm, half):
        h = h_ref[r0:r0 + half, :]
        act = (_silu(_dot(h, wg_ref[...])) * _dot(h, wu_ref[...])).astype(BF16)
        o_ref[r0:r0 + half, :] += _dot(act, wd_ref[...])

    @pl.when(j == pl.num_programs(1) - 1)
    def _():
        x1_copy().wait()
        x2 = x1_buf[...] + gt_ref[...] * o_ref[...]
        o_ref[...] = _rms(x2) * gf_ref[...]


def _ffn(h2, wg, wu, wd, x1, gt, gf, seq):
    t, d = x1.shape
    dff = wg.shape[1]
    tm, tf = 1024, 512
    per = seq // tm
    return pl.pallas_call(
        functools.partial(_ffn_kernel, tm=tm),
        grid=(t // tm, dff // tf),
        in_specs=[pl.BlockSpec((tm, d), lambda i, j: (i, 0)),
                  pl.BlockSpec((d, tf), lambda i, j: (0, j)),
                  pl.BlockSpec((d, tf), lambda i, j: (0, j)),
                  pl.BlockSpec((tf, d), lambda i, j: (j, 0)),
                  pl.BlockSpec(memory_space=pl.ANY),
                  pl.BlockSpec((None, 1, d), lambda i, j: (i // per, 0, 0)),
                  pl.BlockSpec((1, d), lambda i, j: (0, 0))],
        out_specs=pl.BlockSpec((tm, d), lambda i, j: (i, 0)),
        out_shape=jax.ShapeDtypeStruct((t, d), F32),
        scratch_shapes=[pltpu.VMEM((tm, d), F32), pltpu.SemaphoreType.DMA(())],
        compiler_params=_cparams("arbitrary", "arbitrary"),
        name="ffn",
    )(h2, wg, wu, wd, x1, gt, gf.reshape(1, d))


def kernel(x, c, w_ada, b_ada, norm_mix_g, w_in, conv_w, lb_param, gnorm_g, w_conv_out,
           w_hgrn_out, w_o, norm_ffn_g, w_ffn_gate, w_ffn_up, w_ffn_down, norm_final_g):
    b, s, d = x.shape
    t = b * s
    depth = w_in.shape[0]
    dc = conv_w.shape[2]
    dk = lb_param.shape[1]
    dv = w_hgrn_out.shape[1]
    assert dk == HEADS * HEAD_DIM and dv == HEADS * HEAD_DIM and s % CHUNK == 0
    tm = 1024
    cw = 256
    assert s % tm == 0 and dc % cw == 0

    assert depth == 1
    l = 0

    mod = _modulation(c, w_ada[l], b_ada[l]).reshape(b, 6, 1, d)
    sh_m, sc_m, gt_m, sh_f, sc_f, gt_f = [mod[:, i] for i in range(6)]

    h = _hnorm(x, norm_mix_g[l], sc_m, sh_m).reshape(t, d)

    w = w_in[l]
    c_q = 3 * dc
    c_f, c_i, c_g, c_gate = c_q + dk, c_q + 2 * dk, c_q + 2 * dk + dv, c_q + 2 * dk + 2 * dv
    tg = 1024
    assert (w.shape[1] - c_gate) % tg == 0 and c_gate % tg == 0 and c_q % dk == 0 and dk == dv

    z, = _proj_call(
        functools.partial(_conv_epilogue, tm=tm, cw=cw, tiles_per_seq=s // tm),
        h, w, [0, dc, 2 * dc], cw, dc // cw, [conv_w[l]],
        [pl.BlockSpec((CONV_WIDTH, cw), lambda j, i: (0, j))], [cw], [BF16], tm, "proj_conv",
        scratch_shapes=[pltpu.VMEM((tm + SUBLANES, cw), F32)])
    q, = _proj_call(_silu_epilogue, h, w, [c_q], dk, 1, [], [], [dk], [BF16], tm, "proj_q")
    kk, lf = _proj_call(
        functools.partial(_forget_epilogue, layer=l), h, w, [c_f], dk, 1, [lb_param],
        [pl.BlockSpec((lb_param.shape[0], dk), lambda j, i: (0, 0))],
        [dk, dk], [BF16, F32], tm, "proj_f")
    v, = _proj_call(_ident_epilogue, h, w, [c_i], dv, 1, [], [], [dv], [BF16], tm, "proj_i")
    gs, = _proj_call(_gate_gain_epilogue, h, w, [c_g], dv, 1,
                     [jnp.tile(gnorm_g[l], HEADS).reshape(1, dv)],
                     [pl.BlockSpec((1, dv), lambda j, i: (0, 0))], [dv], [BF16], tm, "proj_g")
    sg, = _proj_call(_sigmoid_epilogue, h, w, [c_gate], tg, (w.shape[1] - c_gate) // tg, [], [],
                     [tg], [BF16], tm, "proj_gate")

    on = _hgrn(q, kk, lf, v, gs, b, s)

    x1, h2 = _mix(z, on, sg, x.reshape(t, d), w_conv_out[l].astype(BF16),
                  w_hgrn_out[l].astype(BF16), w_o[l].astype(BF16), gt_m, norm_ffn_g[l],
                  sc_f, sh_f, s)
    out = _ffn(h2, w_ffn_gate[l].astype(BF16), w_ffn_up[l].astype(BF16),
               w_ffn_down[l].astype(BF16), x1, gt_f, norm_final_g, s)
    return out.reshape(b, s, d)
```

```python
import functools

import jax
import jax.numpy as jnp
import numpy as np
from jax import lax
from jax.experimental import pallas as pl
from jax.experimental.pallas import tpu as pltpu

F32 = jnp.float32
BF16 = jnp.bfloat16

EPS = 1e-6
CONV_WIDTH = 3
HEADS = 8
HEAD_DIM = 128
CHUNK = 128
LEVELS = 8
FINE = 3
SUBLANES = 8
MIX_PIECE = 256
PROJ_PIECE = 256
VMEM_LIMIT = 56 * 1024 * 1024


def _cparams(*sem):
    return pltpu.CompilerParams(dimension_semantics=sem, vmem_limit_bytes=VMEM_LIMIT)


def _dot(a, b):
    return jnp.dot(a, b, preferred_element_type=F32)


def _dot_nt(a, b):
    return lax.dot_general(a, b, (((1,), (1,)), ((), ())), preferred_element_type=F32)


def _silu(x):
    return x * jax.nn.sigmoid(x)


def _rms(x):
    return x * lax.rsqrt(jnp.mean(x * x, axis=-1, keepdims=True) + EPS)


def _mod_kernel(c_ref, w_ref, b_ref, o_ref):
    c = c_ref[...]
    o_ref[...] = jnp.dot(_silu(c), w_ref[...], preferred_element_type=F32,
                         precision=lax.Precision.HIGHEST) + b_ref[...]


def _modulation(c, w_ada, b_ada):
    b, d = c.shape
    n = w_ada.shape[1]
    tn = 1024
    return pl.pallas_call(
        _mod_kernel,
        grid=(n // tn,),
        in_specs=[pl.BlockSpec((b, d), lambda j: (0, 0)),
                  pl.BlockSpec((d, tn), lambda j: (0, j)),
                  pl.BlockSpec((1, tn), lambda j: (0, j))],
        out_specs=pl.BlockSpec((b, tn), lambda j: (0, j)),
        out_shape=jax.ShapeDtypeStruct((b, n), F32),
        compiler_params=_cparams("parallel"),
        name="mod",
    )(c, w_ada, b_ada.reshape(1, n))


def _hnorm_kernel(x_ref, g_ref, sc_ref, sh_ref, o_ref):
    y = _rms(x_ref[...]) * g_ref[...]
    o_ref[...] = (y * (1.0 + sc_ref[...]) + sh_ref[...]).astype(BF16)


def _hnorm(x, g, sc, sh):
    b, s, d = x.shape
    tr = 512
    vec = pl.BlockSpec((None, 1, d), lambda bi, r: (bi, 0, 0))
    return pl.pallas_call(
        _hnorm_kernel,
        grid=(b, s // tr),
        in_specs=[pl.BlockSpec((None, tr, d), lambda bi, r: (bi, r, 0)),
                  pl.BlockSpec((1, d), lambda bi, r: (0, 0)), vec, vec],
        out_specs=pl.BlockSpec((None, tr, d), lambda bi, r: (bi, r, 0)),
        out_shape=jax.ShapeDtypeStruct((b, s, d), BF16),
        compiler_params=_cparams("parallel", "parallel"),
        name="hnorm",
    )(x, g.reshape(1, d), sc, sh)


def _proj_kernel(*refs, n_w, n_extra, n_out, epilogue):
    h_ref = refs[0]
    w_refs = refs[1:1 + n_w]
    extra = refs[1 + n_w:1 + n_w + n_extra]
    outs = refs[1 + n_w + n_extra:1 + n_w + n_extra + n_out]
    wbf = refs[1 + n_w + n_extra + n_out]
    scratch = refs[2 + n_w + n_extra + n_out:]

    @pl.when(pl.program_id(1) == 0)
    def _():
        off = 0
        for w_ref in w_refs:
            wd = w_ref.shape[1]
            wbf[:, off:off + wd] = w_ref[...].astype(BF16)
            off += wd

    tm = h_ref.shape[0]
    for r0 in range(0, tm, PROJ_PIECE):
        rows = slice(r0, r0 + PROJ_PIECE)
        epilogue(_dot(h_ref[rows, :], wbf[...]), rows, tm, extra, outs, scratch)


def _proj_call(epilogue, h, w, col_starts, wd, nj, extras, extra_specs, out_widths, out_dtypes,
               tm, name, scratch_shapes=()):
    t, d = h.shape
    in_specs = [pl.BlockSpec((tm, d), lambda j, i: (i, 0))]
    for c0 in col_starts:
        in_specs.append(pl.BlockSpec((d, wd), functools.partial(lambda j, i, b: (0, b + j), b=c0 // wd)))
    in_specs += list(extra_specs)
    out_specs = [pl.BlockSpec((tm, ow), lambda j, i: (i, j)) for ow in out_widths]
    out_shape = [jax.ShapeDtypeStruct((t, nj * ow), dt) for ow, dt in zip(out_widths, out_dtypes)]
    kern = functools.partial(_proj_kernel, n_w=len(col_starts), n_extra=len(extras),
                             n_out=len(out_widths), epilogue=epilogue)
    return pl.pallas_call(
        kern,
        grid=(nj, t // tm),
        in_specs=in_specs,
        out_specs=out_specs,
        out_shape=out_shape,
        scratch_shapes=[pltpu.VMEM((d, wd * len(col_starts)), BF16)] + list(scratch_shapes),
        compiler_params=_cparams("arbitrary", "arbitrary"),
        name=name,
    )(h, *([w] * len(col_starts)), *extras)


def _conv_epilogue(acc, rows, tm, extra, outs, scratch, *, cw, tiles_per_seq):
    cw_ref, = extra
    o_ref, = outs
    ubuf, = scratch
    r0, n = rows.start, rows.stop - rows.start
    u = acc[:, cw:2 * cw] * acc[:, 2 * cw:]

    if r0 == 0:
        @pl.when(pl.program_id(1) % tiles_per_seq == 0)
        def _():
            ubuf[0:SUBLANES, :] = jnp.zeros((SUBLANES, cw), F32)

    ubuf[SUBLANES + r0:SUBLANES + r0 + n, :] = u
    wk = cw_ref[...]
    y = wk[2:3, :] * u
    y = y + wk[1:2, :] * ubuf[SUBLANES - 1 + r0:SUBLANES - 1 + r0 + n, :]
    y = y + wk[0:1, :] * ubuf[SUBLANES - 2 + r0:SUBLANES - 2 + r0 + n, :]
    o_ref[rows, :] = (acc[:, :cw] * y).astype(BF16)
    if rows.stop == tm:
        ubuf[0:SUBLANES, :] = ubuf[tm:tm + SUBLANES, :]


def _silu_epilogue(acc, rows, tm, extra, outs, scratch):
    outs[0][rows, :] = _silu(acc).astype(BF16)


def _gate_gain_epilogue(acc, rows, tm, extra, outs, scratch):
    outs[0][rows, :] = (_silu(acc) * extra[0][...]).astype(BF16)


def _ident_epilogue(acc, rows, tm, extra, outs, scratch):
    outs[0][rows, :] = acc.astype(BF16)


def _sigmoid_epilogue(acc, rows, tm, extra, outs, scratch):
    outs[0][rows, :] = jax.nn.sigmoid(acc).astype(BF16)


def _forget_epilogue(acc, rows, tm, extra, outs, scratch, *, layer):
    k_ref, lf_ref = outs
    p = extra[0][...]
    e = jnp.exp(p - jnp.max(p, axis=0, keepdims=True))
    sm = e / jnp.sum(e, axis=0, keepdims=True)
    lb = jnp.sum(sm[:layer + 1, :], axis=0, keepdims=True)
    f = lb + (1.0 - lb) * jax.nn.sigmoid(acc)
    k_ref[rows, :] = (1.0 - f).astype(BF16)
    lf_ref[rows, :] = jnp.log2(f)


def _level_tables():
    n = CHUNK
    level = -np.ones((n, n), np.int32)
    expo = np.zeros((FINE + 1, n, n), np.float32)
    for t in range(n):
        level[t, t] = 0
        for s in range(t):
            level[t, s] = (t ^ s).bit_length()
        for li in range(1, FINE + 1):
            m = 1 << (li - 1)
            blk = (t // m) * m
            if (t % (2 * m)) >= m:
                expo[li - 1, t, blk:t + 1] = 1.0
            else:
                expo[li - 1, t, t + 1:blk + m] = 1.0
        expo[FINE, t, :t + 1] = 1.0
    expo = expo.reshape((FINE + 1) * n, n)
    return level, np.concatenate([expo, expo], axis=1)


def _hgrn_kernel(q_ref, k_ref, lf_ref, v_ref, gs_ref, lv_ref, ex_ref, o_ref, st_ref, dec_ref):
    c = pl.program_id(1)

    @pl.when(c == 0)
    def _():
        st_ref[...] = jnp.zeros_like(st_ref)

    n = CHUNK
    nb = n // SUBLANES
    g0 = FINE * n
    tile = lambda a, r: a[r * SUBLANES:(r + 1) * SUBLANES, :]

    lf = lf_ref[...]
    hi = lf.astype(BF16)
    lo = (lf - hi.astype(F32)).astype(BF16)
    dec_ref[...] = _dot(ex_ref[...], jnp.concatenate([hi, lo], axis=0))

    for hd in range(HEADS):
        sl = slice(hd * HEAD_DIM, (hd + 1) * HEAD_DIM)
        qb = q_ref[:, sl]
        kb = k_ref[:, sl]
        vb = v_ref[:, sl]
        q = qb.astype(F32)
        k = kb.astype(F32)
        g = dec_ref[g0:g0 + n, sl]

        def halves_exponent(m):
            parts = []
            for r0 in range(0, n, 2 * m):
                g_n = dec_ref[g0 + r0 + m - 1:g0 + r0 + m, sl]
                parts += [g_n - g[r0:r0 + m], g[r0 + m:r0 + 2 * m] - g_n]
            return jnp.concatenate(parts, axis=0)

        p = _dot_nt(qb, kb)
        sc = [jnp.where(tile(lv_ref, r) == 0, tile(p, r), 0.0) for r in range(nb)]
        for li in range(1, LEVELS):
            m = 1 << (li - 1)
            x = dec_ref[(li - 1) * n:li * n, sl] if li <= FINE else halves_exponent(m)
            e = jnp.exp2(x)
            if m < SUBLANES:
                rows = list(range(nb))
                p = _dot_nt((q * e).astype(BF16), (k * e).astype(BF16))
            else:
                q_parts, k_parts, rows = [], [], []
                for r0 in range(0, n, 2 * m):
                    k_parts += [k[r0:r0 + m] * e[r0:r0 + m], jnp.zeros((m, HEAD_DIM), F32)]
                    q_parts += [q[r0 + m:r0 + 2 * m] * e[r0 + m:r0 + 2 * m]]
                    rows += list(range((r0 + m) // SUBLANES, (r0 + 2 * m) // SUBLANES))
                p = _dot_nt(jnp.concatenate(q_parts, axis=0).astype(BF16),
                            jnp.concatenate(k_parts, axis=0).astype(BF16))
            for i, r in enumerate(rows):
                sc[r] = jnp.where(tile(lv_ref, r) == li, tile(p, i), sc[r])
        scores = jnp.concatenate(sc, axis=0).astype(BF16)
        g_last = dec_ref[g0 + n - 1:g0 + n, sl]
        q_in = (q * jnp.exp2(g)).astype(BF16)
        k_out = (k * jnp.exp2(g_last - g)).astype(BF16)
        st = st_ref[hd]
        o = _dot_nt(q_in, st.astype(BF16)) + _dot(scores, vb)
        vt = vb.astype(F32).T.astype(BF16)
        st_ref[hd] = st * jnp.exp2(g_last) + _dot(vt, k_out)
        o_ref[:, sl] = (_rms(o) * gs_ref[:, sl].astype(F32)).astype(BF16)


def _hgrn(q, k, lf, v, gs, batch, seq):
    t, dk = q.shape
    nc = seq // CHUNK
    level, expo = _level_tables()
    row = lambda b, c: (b * nc + c, 0)
    const = lambda b, c: (0, 0)
    blk = pl.BlockSpec((CHUNK, dk), row)
    return pl.pallas_call(
        _hgrn_kernel,
        grid=(batch, nc),
        in_specs=[blk, blk, blk, blk, blk,
                  pl.BlockSpec((CHUNK, CHUNK), const),
                  pl.BlockSpec(expo.shape, const)],
        out_specs=blk,
        out_shape=jax.ShapeDtypeStruct((t, dk), BF16),
        scratch_shapes=[pltpu.VMEM((HEADS, HEAD_DIM, HEAD_DIM), F32),
                        pltpu.VMEM(((FINE + 1) * CHUNK, dk), F32)],
        compiler_params=_cparams("arbitrary", "arbitrary"),
        name="hgrn",
    )(q, k, lf, v, gs, jnp.asarray(level), jnp.asarray(expo, BF16))


def _mix_kernel(z_ref, on_ref, sg_ref, x_ref, wc_ref, wh_ref, wo_ref, gt_ref, g2_ref, sc_ref,
                sh_ref, x1_ref, h2_ref, *, d):
    for r0 in range(0, z_ref.shape[0], MIX_PIECE):
        rows = slice(r0, r0 + MIX_PIECE)
        y_a = _dot(z_ref[rows, :], wc_ref[...])
        y_b = _dot(on_ref[rows, :], wh_ref[...])
        merged = sg_ref[rows, :d].astype(F32) * y_a + sg_ref[rows, d:].astype(F32) * y_b
        x1 = x_ref[rows, :] + gt_ref[...] * _dot(merged.astype(BF16), wo_ref[...])
        x1_ref[rows, :] = x1
        h2_ref[rows, :] = (_rms(x1) * g2_ref[...] * (1.0 + sc_ref[...]) + sh_ref[...]).astype(BF16)


def _mix(z, on, sg, x, wc, wh, wo, gt, g2, sc, sh, seq):
    t, d = x.shape
    tm = 512
    assert tm % MIX_PIECE == 0
    per = seq // tm
    row = lambda i: (i, 0)
    const = lambda i: (0, 0)
    vec = pl.BlockSpec((None, 1, d), lambda i: (i // per, 0, 0))
    resident = lambda shape: pl.BlockSpec(shape, const, pipeline_mode=pl.Buffered(1))
    return pl.pallas_call(
        functools.partial(_mix_kernel, d=d),
        grid=(t // tm,),
        in_specs=[pl.BlockSpec((tm, z.shape[1]), row),
                  pl.BlockSpec((tm, on.shape[1]), row),
                  pl.BlockSpec((tm, 2 * d), row),
                  pl.BlockSpec((tm, d), row),
                  resident(wc.shape), resident(wh.shape), resident(wo.shape),
                  vec, pl.BlockSpec((1, d), const), vec, vec],
        out_specs=[pl.BlockSpec((tm, d), row), pl.BlockSpec((tm, d), row)],
        out_shape=[jax.ShapeDtypeStruct((t, d), F32), jax.ShapeDtypeStruct((t, d), BF16)],
        compiler_params=_cparams("parallel"),
        name="mix",
    )(z, on, sg, x, wc, wh, wo, gt, g2.reshape(1, d), sc, sh)


def _ffn_kernel(h_ref, wg_ref, wu_ref, wd_ref, x1_hbm, gt_ref, gf_ref, o_ref, x1_buf, x1_sem, *, tm):
    i = pl.program_id(0)
    j = pl.program_id(1)

    def x1_copy():
        return pltpu.make_async_copy(x1_hbm.at[pl.ds(i * tm, tm), :], x1_buf, x1_sem)

    @pl.when(j == 0)
    def _():
        x1_copy().start()
        o_ref[...] = jnp.zeros_like(o_ref)

    half = tm // 2
    for r0 in range(0, tm, half):
        h = h_ref[r0:r0 + half, :]
        act = (_silu(_dot(h, wg_ref[...])) * _dot(h, wu_ref[...])).astype(BF16)
        o_ref[r0:r0 + half, :] += _dot(act, wd_ref[...])

    @pl.when(j == pl.num_programs(1) - 1)
    def _():
        x1_copy().wait()
        x2 = x1_buf[...] + gt_ref[...] * o_ref[...]
        o_ref[...] = _rms(x2) * gf_ref[...]


def _ffn(h2, wg, wu, wd, x1, gt, gf, seq):
    t, d = x1.shape
    dff = wg.shape[1]
    tm, tf = 1024, 512
    per = seq // tm
    return pl.pallas_call(
        functools.partial(_ffn_kernel, tm=tm),
        grid=(t // tm, dff // tf),
        in_specs=[pl.BlockSpec((tm, d), lambda i, j: (i, 0)),
                  pl.BlockSpec((d, tf), lambda i, j: (0, j)),
                  pl.BlockSpec((d, tf), lambda i, j: (0, j)),
                  pl.BlockSpec((tf, d), lambda i, j: (j, 0)),
                  pl.BlockSpec(memory_space=pl.ANY),
                  pl.BlockSpec((None, 1, d), lambda i, j: (i // per, 0, 0)),
                  pl.BlockSpec((1, d), lambda i, j: (0, 0))],
        out_specs=pl.BlockSpec((tm, d), lambda i, j: (i, 0)),
        out_shape=jax.ShapeDtypeStruct((t, d), F32),
        scratch_shapes=[pltpu.VMEM((tm, d), F32), pltpu.SemaphoreType.DMA(())],
        compiler_params=_cparams("arbitrary", "arbitrary"),
        name="ffn",
    )(h2, wg, wu, wd, x1, gt, gf.reshape(1, d))


def kernel(x, c, w_ada, b_ada, norm_mix_g, w_in, conv_w, lb_param, gnorm_g, w_conv_out,
           w_hgrn_out, w_o, norm_ffn_g, w_ffn_gate, w_ffn_up, w_ffn_down, norm_final_g):
    b, s, d = x.shape
    t = b * s
    depth = w_in.shape[0]
    dc = conv_w.shape[2]
    dk = lb_param.shape[1]
    dv = w_hgrn_out.shape[1]
    assert dk == HEADS * HEAD_DIM and dv == HEADS * HEAD_DIM and s % CHUNK == 0
    tm = 2048
    tm_conv = 1024
    cw = 512
    assert s % tm == 0 and s % tm_conv == 0 and dc % cw == 0
    assert tm % PROJ_PIECE == 0 and tm_conv % PROJ_PIECE == 0

    assert depth == 1
    l = 0

    mod = _modulation(c, w_ada[l], b_ada[l]).reshape(b, 6, 1, d)
    sh_m, sc_m, gt_m, sh_f, sc_f, gt_f = [mod[:, i] for i in range(6)]

    h = _hnorm(x, norm_mix_g[l], sc_m, sh_m).reshape(t, d)

    w = w_in[l]
    c_q = 3 * dc
    c_f, c_i, c_g, c_gate = c_q + dk, c_q + 2 * dk, c_q + 2 * dk + dv, c_q + 2 * dk + 2 * dv
    tg = 1024
    assert (w.shape[1] - c_gate) % tg == 0 and c_gate % tg == 0 and c_q % dk == 0 and dk == dv

    z, = _proj_call(
        functools.partial(_conv_epilogue, cw=cw, tiles_per_seq=s // tm_conv),
        h, w, [0, dc, 2 * dc], cw, dc // cw, [conv_w[l]],
        [pl.BlockSpec((CONV_WIDTH, cw), lambda j, i: (0, j))], [cw], [BF16], tm_conv, "proj_conv",
        scratch_shapes=[pltpu.VMEM((tm_conv + SUBLANES, cw), F32)])
    q, = _proj_call(_silu_epilogue, h, w, [c_q], dk, 1, [], [], [dk], [BF16], tm, "proj_q")
    kk, lf = _proj_call(
        functools.partial(_forget_epilogue, layer=l), h, w, [c_f], dk, 1, [lb_param],
        [pl.BlockSpec((lb_param.shape[0], dk), lambda j, i: (0, 0))],
        [dk, dk], [BF16, F32], tm, "proj_f")
    v, = _proj_call(_ident_epilogue, h, w, [c_i], dv, 1, [], [], [dv], [BF16], tm, "proj_i")
    gs, = _proj_call(_gate_gain_epilogue, h, w, [c_g], dv, 1,
                     [jnp.tile(gnorm_g[l], HEADS).reshape(1, dv)],
                     [pl.BlockSpec((1, dv), lambda j, i: (0, 0))], [dv], [BF16], tm, "proj_g")
    sg, = _proj_call(_sigmoid_epilogue, h, w, [c_gate], tg, (w.shape[1] - c_gate) // tg, [], [],
                     [tg], [BF16], tm, "proj_gate")

    on = _hgrn(q, kk, lf, v, gs, b, s)

    x1, h2 = _mix(z, on, sg, x.reshape(t, d), w_conv_out[l].astype(BF16),
                  w_hgrn_out[l].astype(BF16), w_o[l].astype(BF16), gt_m, norm_ffn_g[l],
                  sc_f, sh_f, s)
    out = _ffn(h2, w_ffn_gate[l].astype(BF16), w_ffn_up[l].astype(BF16),
               w_ffn_down[l].astype(BF16), x1, gt_f, norm_final_g, s)
    return out.reshape(b, s, d)
```

```python
import functools

import jax
import jax.numpy as jnp
import numpy as np
from jax import lax
from jax.experimental import pallas as pl
from jax.experimental.pallas import tpu as pltpu

F32 = jnp.float32
BF16 = jnp.bfloat16

EPS = 1e-6
CONV_WIDTH = 3
HEADS = 8
HEAD_DIM = 128
CHUNK = 128
LEVELS = 8
FINE = 3
SUBLANES = 8
MIX_PIECE = 256
PROJ_PIECE = 256
VMEM_LIMIT = 56 * 1024 * 1024


def _cparams(*sem):
    return pltpu.CompilerParams(dimension_semantics=sem, vmem_limit_bytes=VMEM_LIMIT)


def _dot(a, b):
    return jnp.dot(a, b, preferred_element_type=F32)


def _dot_nt(a, b):
    return lax.dot_general(a, b, (((1,), (1,)), ((), ())), preferred_element_type=F32)


def _silu(x):
    return x * jax.nn.sigmoid(x)


def _rms(x):
    return x * lax.rsqrt(jnp.mean(x * x, axis=-1, keepdims=True) + EPS)


def _mod_kernel(c_ref, w_ref, b_ref, o_ref):
    c = c_ref[...]
    o_ref[...] = jnp.dot(_silu(c), w_ref[...], preferred_element_type=F32,
                         precision=lax.Precision.HIGHEST) + b_ref[...]


def _modulation(c, w_ada, b_ada):
    b, d = c.shape
    n = w_ada.shape[1]
    tn = 1024
    return pl.pallas_call(
        _mod_kernel,
        grid=(n // tn,),
        in_specs=[pl.BlockSpec((b, d), lambda j: (0, 0)),
                  pl.BlockSpec((d, tn), lambda j: (0, j)),
                  pl.BlockSpec((1, tn), lambda j: (0, j))],
        out_specs=pl.BlockSpec((b, tn), lambda j: (0, j)),
        out_shape=jax.ShapeDtypeStruct((b, n), F32),
        compiler_params=_cparams("parallel"),
        name="mod",
    )(c, w_ada, b_ada.reshape(1, n))


def _rows_lhs(h_ref, rows, extra, outs):
    return h_ref[rows, :]


def _norm_lhs(x_ref, rows, extra, outs):
    g_ref, sc_ref, sh_ref = extra
    y = _rms(x_ref[rows, :]) * g_ref[...]
    h = (y * (1.0 + sc_ref[...]) + sh_ref[...]).astype(BF16)
    outs[-1][rows, :] = h
    return h


def _proj_kernel(*refs, n_w, n_extra, n_side, n_out, epilogue, lhs):
    src_ref = refs[0]
    pos = 1
    w_refs = refs[pos:pos + n_w]
    pos += n_w
    extra = refs[pos:pos + n_extra]
    pos += n_extra
    side_in = refs[pos:pos + n_side]
    pos += n_side
    outs = refs[pos:pos + n_out]
    pos += n_out
    side_out = refs[pos:pos + n_side]
    pos += n_side
    wbf = refs[pos]
    scratch = refs[pos + 1:]

    @pl.when(pl.program_id(1) == 0)
    def _():
        off = 0
        for w_ref in w_refs:
            wd = w_ref.shape[1]
            wbf[:, off:off + wd] = w_ref[...].astype(BF16)
            off += wd

    for s_ref, o_ref in zip(side_in, side_out):
        o_ref[...] = s_ref[...].astype(BF16)

    tm = src_ref.shape[0]
    for r0 in range(0, tm, PROJ_PIECE):
        rows = slice(r0, r0 + PROJ_PIECE)
        epilogue(_dot(lhs(src_ref, rows, extra, outs), wbf[...]), rows, tm, extra, outs, scratch)


def _proj_call(epilogue, h, w, col_starts, wd, nj, extras, extra_specs, out_widths, out_dtypes,
               tm, name, scratch_shapes=(), lhs=_rows_lhs, side_casts=()):
    t, d = h.shape
    ni = t // tm
    in_specs = [pl.BlockSpec((tm, d), lambda j, i: (i, 0))]
    for c0 in col_starts:
        in_specs.append(pl.BlockSpec((d, wd), functools.partial(lambda j, i, b: (0, b + j), b=c0 // wd)))
    in_specs += list(extra_specs)
    out_specs = [pl.BlockSpec((tm, ow), lambda j, i: (i, j)) for ow in out_widths]
    out_shape = [jax.ShapeDtypeStruct((t, nj * ow), dt) for ow, dt in zip(out_widths, out_dtypes)]
    for a in side_casts:
        rb = a.shape[0] // (nj * ni)
        assert rb * nj * ni == a.shape[0] and rb % (2 * SUBLANES) == 0
        spec = pl.BlockSpec((rb, a.shape[1]), lambda j, i: (j * ni + i, 0))
        in_specs.append(spec)
        out_specs.append(spec)
        out_shape.append(jax.ShapeDtypeStruct(a.shape, BF16))
    kern = functools.partial(_proj_kernel, n_w=len(col_starts), n_extra=len(extras),
                             n_side=len(side_casts), n_out=len(out_widths), epilogue=epilogue,
                             lhs=lhs)
    return pl.pallas_call(
        kern,
        grid=(nj, ni),
        in_specs=in_specs,
        out_specs=out_specs,
        out_shape=out_shape,
        scratch_shapes=[pltpu.VMEM((d, wd * len(col_starts)), BF16)] + list(scratch_shapes),
        compiler_params=_cparams("arbitrary", "arbitrary"),
        name=name,
    )(h, *([w] * len(col_starts)), *extras, *side_casts)


def _conv_epilogue(acc, rows, tm, extra, outs, scratch, *, cw, tiles_per_seq):
    cw_ref, = extra
    o_ref, = outs
    ubuf, = scratch
    r0, n = rows.start, rows.stop - rows.start
    u = acc[:, cw:2 * cw] * acc[:, 2 * cw:]

    if r0 == 0:
        @pl.when(pl.program_id(1) % tiles_per_seq == 0)
        def _():
            ubuf[0:SUBLANES, :] = jnp.zeros((SUBLANES, cw), F32)

    ubuf[SUBLANES + r0:SUBLANES + r0 + n, :] = u
    wk = cw_ref[...]
    y = wk[2:3, :] * u
    y = y + wk[1:2, :] * ubuf[SUBLANES - 1 + r0:SUBLANES - 1 + r0 + n, :]
    y = y + wk[0:1, :] * ubuf[SUBLANES - 2 + r0:SUBLANES - 2 + r0 + n, :]
    o_ref[rows, :] = (acc[:, :cw] * y).astype(BF16)
    if rows.stop == tm:
        ubuf[0:SUBLANES, :] = ubuf[tm:tm + SUBLANES, :]


def _silu_epilogue(acc, rows, tm, extra, outs, scratch):
    outs[0][rows, :] = _silu(acc).astype(BF16)


def _gate_gain_epilogue(acc, rows, tm, extra, outs, scratch):
    outs[0][rows, :] = (_silu(acc) * extra[0][...]).astype(BF16)


def _ident_epilogue(acc, rows, tm, extra, outs, scratch):
    outs[0][rows, :] = acc.astype(BF16)


def _sigmoid_epilogue(acc, rows, tm, extra, outs, scratch):
    outs[0][rows, :] = jax.nn.sigmoid(acc).astype(BF16)


def _forget_epilogue(acc, rows, tm, extra, outs, scratch, *, layer):
    k_ref, lf_ref = outs
    p = extra[0][...]
    e = jnp.exp(p - jnp.max(p, axis=0, keepdims=True))
    sm = e / jnp.sum(e, axis=0, keepdims=True)
    lb = jnp.sum(sm[:layer + 1, :], axis=0, keepdims=True)
    f = lb + (1.0 - lb) * jax.nn.sigmoid(acc)
    k_ref[rows, :] = (1.0 - f).astype(BF16)
    lf_ref[rows, :] = jnp.log2(f)


def _level_tables():
    n = CHUNK
    level = -np.ones((n, n), np.int32)
    expo = np.zeros((FINE + 1, n, n), np.float32)
    for t in range(n):
        level[t, t] = 0
        for s in range(t):
            level[t, s] = (t ^ s).bit_length()
        for li in range(1, FINE + 1):
            m = 1 << (li - 1)
            blk = (t // m) * m
            if (t % (2 * m)) >= m:
                expo[li - 1, t, blk:t + 1] = 1.0
            else:
                expo[li - 1, t, t + 1:blk + m] = 1.0
        expo[FINE, t, :t + 1] = 1.0
    expo = expo.reshape((FINE + 1) * n, n)
    return level, np.concatenate([expo, expo], axis=1)


def _hgrn_kernel(q_ref, k_ref, lf_ref, v_ref, gs_ref, lv_ref, ex_ref, o_ref, st_ref, dec_ref):
    c = pl.program_id(1)

    @pl.when(c == 0)
    def _():
        st_ref[...] = jnp.zeros_like(st_ref)

    n = CHUNK
    nb = n // SUBLANES
    g0 = FINE * n
    tile = lambda a, r: a[r * SUBLANES:(r + 1) * SUBLANES, :]

    lf = lf_ref[...]
    hi = lf.astype(BF16)
    lo = (lf - hi.astype(F32)).astype(BF16)
    dec_ref[...] = _dot(ex_ref[...], jnp.concatenate([hi, lo], axis=0))

    for hd in range(HEADS):
        sl = slice(hd * HEAD_DIM, (hd + 1) * HEAD_DIM)
        qb = q_ref[:, sl]
        kb = k_ref[:, sl]
        vb = v_ref[:, sl]
        q = qb.astype(F32)
        k = kb.astype(F32)
        g = dec_ref[g0:g0 + n, sl]

        def halves_exponent(m):
            parts = []
            for r0 in range(0, n, 2 * m):
                g_n = dec_ref[g0 + r0 + m - 1:g0 + r0 + m, sl]
                parts += [g_n - g[r0:r0 + m], g[r0 + m:r0 + 2 * m] - g_n]
            return jnp.concatenate(parts, axis=0)

        p = _dot_nt(qb, kb)
        sc = [jnp.where(tile(lv_ref, r) == 0, tile(p, r), 0.0) for r in range(nb)]
        for li in range(1, LEVELS):
            m = 1 << (li - 1)
            x = dec_ref[(li - 1) * n:li * n, sl] if li <= FINE else halves_exponent(m)
            e = jnp.exp2(x)
            if m < SUBLANES:
                rows = list(range(nb))
                p = _dot_nt((q * e).astype(BF16), (k * e).astype(BF16))
            else:
                q_parts, k_parts, rows = [], [], []
                for r0 in range(0, n, 2 * m):
                    k_parts += [k[r0:r0 + m] * e[r0:r0 + m], jnp.zeros((m, HEAD_DIM), F32)]
                    q_parts += [q[r0 + m:r0 + 2 * m] * e[r0 + m:r0 + 2 * m]]
                    rows += list(range((r0 + m) // SUBLANES, (r0 + 2 * m) // SUBLANES))
                p = _dot_nt(jnp.concatenate(q_parts, axis=0).astype(BF16),
                            jnp.concatenate(k_parts, axis=0).astype(BF16))
            for i, r in enumerate(rows):
                sc[r] = jnp.where(tile(lv_ref, r) == li, tile(p, i), sc[r])
        scores = jnp.concatenate(sc, axis=0).astype(BF16)
        g_last = dec_ref[g0 + n - 1:g0 + n, sl]
        q_in = (q * jnp.exp2(g)).astype(BF16)
        k_out = (k * jnp.exp2(g_last - g)).astype(BF16)
        st = st_ref[hd]
        o = _dot_nt(q_in, st.astype(BF16)) + _dot(scores, vb)
        vt = vb.astype(F32).T.astype(BF16)
        st_ref[hd] = st * jnp.exp2(g_last) + _dot(vt, k_out)
        o_ref[:, sl] = (_rms(o) * gs_ref[:, sl].astype(F32)).astype(BF16)


def _hgrn(q, k, lf, v, gs, batch, seq):
    t, dk = q.shape
    nc = seq // CHUNK
    level, expo = _level_tables()
    row = lambda b, c: (b * nc + c, 0)
    const = lambda b, c: (0, 0)
    blk = pl.BlockSpec((CHUNK, dk), row)
    return pl.pallas_call(
        _hgrn_kernel,
        grid=(batch, nc),
        in_specs=[blk, blk, blk, blk, blk,
                  pl.BlockSpec((CHUNK, CHUNK), const),
                  pl.BlockSpec(expo.shape, const)],
        out_specs=blk,
        out_shape=jax.ShapeDtypeStruct((t, dk), BF16),
        scratch_shapes=[pltpu.VMEM((HEADS, HEAD_DIM, HEAD_DIM), F32),
                        pltpu.VMEM(((FINE + 1) * CHUNK, dk), F32)],
        compiler_params=_cparams("arbitrary", "arbitrary"),
        name="hgrn",
    )(q, k, lf, v, gs, jnp.asarray(level), jnp.asarray(expo, BF16))


def _mix_kernel(z_ref, on_ref, sg_ref, x_ref, wc_ref, wh_ref, wo_ref, gt_ref, g2_ref, sc_ref,
                sh_ref, x1_ref, h2_ref, *, d):
    for r0 in range(0, z_ref.shape[0], MIX_PIECE):
        rows = slice(r0, r0 + MIX_PIECE)
        y_a = _dot(z_ref[rows, :], wc_ref[...])
        y_b = _dot(on_ref[rows, :], wh_ref[...])
        merged = sg_ref[rows, :d].astype(F32) * y_a + sg_ref[rows, d:].astype(F32) * y_b
        x1 = x_ref[rows, :] + gt_ref[...] * _dot(merged.astype(BF16), wo_ref[...])
        x1_ref[rows, :] = x1
        h2_ref[rows, :] = (_rms(x1) * g2_ref[...] * (1.0 + sc_ref[...]) + sh_ref[...]).astype(BF16)


def _mix(z, on, sg, x, wc, wh, wo, gt, g2, sc, sh, seq):
    t, d = x.shape
    tm = 512
    assert tm % MIX_PIECE == 0
    per = seq // tm
    row = lambda i: (i, 0)
    const = lambda i: (0, 0)
    vec = pl.BlockSpec((None, 1, d), lambda i: (i // per, 0, 0))
    resident = lambda shape: pl.BlockSpec(shape, const, pipeline_mode=pl.Buffered(1))
    return pl.pallas_call(
        functools.partial(_mix_kernel, d=d),
        grid=(t // tm,),
        in_specs=[pl.BlockSpec((tm, z.shape[1]), row),
                  pl.BlockSpec((tm, on.shape[1]), row),
                  pl.BlockSpec((tm, 2 * d), row),
                  pl.BlockSpec((tm, d), row),
                  resident(wc.shape), resident(wh.shape), resident(wo.shape),
                  vec, pl.BlockSpec((1, d), const), vec, vec],
        out_specs=[pl.BlockSpec((tm, d), row), pl.BlockSpec((tm, d), row)],
        out_shape=[jax.ShapeDtypeStruct((t, d), F32), jax.ShapeDtypeStruct((t, d), BF16)],
        compiler_params=_cparams("parallel"),
        name="mix",
    )(z, on, sg, x, wc, wh, wo, gt, g2.reshape(1, d), sc, sh)


def _ffn_kernel(h_ref, wg_ref, wu_ref, wd_ref, x1_hbm, gt_ref, gf_ref, o_ref, x1_buf, x1_sem, *, tm):
    i = pl.program_id(0)
    j = pl.program_id(1)

    def x1_copy():
        return pltpu.make_async_copy(x1_hbm.at[pl.ds(i * tm, tm), :], x1_buf, x1_sem)

    halves = [slice(r0, r0 + tm // 2) for r0 in range(0, tm, tm // 2)]

    def down(rows):
        h = h_ref[rows, :]
        act = (_silu(_dot(h, wg_ref[...])) * _dot(h, wu_ref[...])).astype(BF16)
        return _dot(act, wd_ref[...])

    last = pl.num_programs(1) - 1

    @pl.when(j == 0)
    def _():
        x1_copy().start()
        for rows in halves:
            o_ref[rows, :] = down(rows)

    @pl.when(jnp.logical_and(j > 0, j < last))
    def _():
        for rows in halves:
            o_ref[rows, :] += down(rows)

    @pl.when(j == last)
    def _():
        x1_copy().wait()
        for rows in halves:
            x2 = x1_buf[rows, :] + gt_ref[...] * (o_ref[rows, :] + down(rows))
            o_ref[rows, :] = _rms(x2) * gf_ref[...]


def _ffn(h2, wg, wu, wd, x1, gt, gf, seq):
    t, d = x1.shape
    dff = wg.shape[1]
    tm, tf = 1024, 512
    per = seq // tm
    return pl.pallas_call(
        functools.partial(_ffn_kernel, tm=tm),
        grid=(t // tm, dff // tf),
        in_specs=[pl.BlockSpec((tm, d), lambda i, j: (i, 0)),
                  pl.BlockSpec((d, tf), lambda i, j: (0, j)),
                  pl.BlockSpec((d, tf), lambda i, j: (0, j)),
                  pl.BlockSpec((tf, d), lambda i, j: (j, 0)),
                  pl.BlockSpec(memory_space=pl.ANY),
                  pl.BlockSpec((None, 1, d), lambda i, j: (i // per, 0, 0)),
                  pl.BlockSpec((1, d), lambda i, j: (0, 0))],
        out_specs=pl.BlockSpec((tm, d), lambda i, j: (i, 0)),
        out_shape=jax.ShapeDtypeStruct((t, d), F32),
        scratch_shapes=[pltpu.VMEM((tm, d), F32), pltpu.SemaphoreType.DMA(())],
        compiler_params=_cparams("arbitrary", "arbitrary"),
        name="ffn",
    )(h2, wg, wu, wd, x1, gt, gf.reshape(1, d))


def kernel(x, c, w_ada, b_ada, norm_mix_g, w_in, conv_w, lb_param, gnorm_g, w_conv_out,
           w_hgrn_out, w_o, norm_ffn_g, w_ffn_gate, w_ffn_up, w_ffn_down, norm_final_g):
    b, s, d = x.shape
    t = b * s
    depth = w_in.shape[0]
    dc = conv_w.shape[2]
    dk = lb_param.shape[1]
    dv = w_hgrn_out.shape[1]
    assert dk == HEADS * HEAD_DIM and dv == HEADS * HEAD_DIM and s % CHUNK == 0
    tm = 2048
    tm_conv = 1024
    tm_norm = 1024
    tm_gate = 1024
    cw = 512
    assert dc % cw == 0 and all(s % r == 0 and r % PROJ_PIECE == 0
                                for r in (tm, tm_conv, tm_norm, tm_gate))

    assert depth == 1
    l = 0

    mod = _modulation(c, w_ada[l], b_ada[l]).reshape(b, 6, 1, d)
    sh_m, sc_m, gt_m, sh_f, sc_f, gt_f = [mod[:, i] for i in range(6)]

    w = w_in[l]
    c_q = 3 * dc
    c_f, c_i, c_g, c_gate = c_q + dk, c_q + 2 * dk, c_q + 2 * dk + dv, c_q + 2 * dk + 2 * dv
    tg = 1024
    assert (w.shape[1] - c_gate) % tg == 0 and c_gate % tg == 0 and c_q % dk == 0 and dk == dv

    x2d = x.reshape(t, d)
    vec = pl.BlockSpec((None, 1, d), lambda j, i: (i // (s // tm_norm), 0, 0))
    q, h = _proj_call(
        _silu_epilogue, x2d, w, [c_q], dk, 1, [norm_mix_g[l].reshape(1, d), sc_m, sh_m],
        [pl.BlockSpec((1, d), lambda j, i: (0, 0)), vec, vec], [dk, d], [BF16, BF16], tm_norm,
        "proj_q", lhs=_norm_lhs)
    z, wd_b = _proj_call(
        functools.partial(_conv_epilogue, cw=cw, tiles_per_seq=s // tm_conv),
        h, w, [0, dc, 2 * dc], cw, dc // cw, [conv_w[l]],
        [pl.BlockSpec((CONV_WIDTH, cw), lambda j, i: (0, j))], [cw], [BF16], tm_conv, "proj_conv",
        scratch_shapes=[pltpu.VMEM((tm_conv + SUBLANES, cw), F32)], side_casts=[w_ffn_down[l]])
    kk, lf = _proj_call(
        functools.partial(_forget_epilogue, layer=l), h, w, [c_f], dk, 1, [lb_param],
        [pl.BlockSpec((lb_param.shape[0], dk), lambda j, i: (0, 0))],
        [dk, dk], [BF16, F32], tm, "proj_f")
    v, wc_b, wh_b = _proj_call(_ident_epilogue, h, w, [c_i], dv, 1, [], [], [dv], [BF16], tm,
                               "proj_i", side_casts=[w_conv_out[l], w_hgrn_out[l]])
    gs, wo_b = _proj_call(_gate_gain_epilogue, h, w, [c_g], dv, 1,
                          [jnp.tile(gnorm_g[l], HEADS).reshape(1, dv)],
                          [pl.BlockSpec((1, dv), lambda j, i: (0, 0))], [dv], [BF16], tm, "proj_g",
                          side_casts=[w_o[l]])
    sg, wg_b, wu_b = _proj_call(_sigmoid_epilogue, h, w, [c_gate], tg, (w.shape[1] - c_gate) // tg,
                                [], [], [tg], [BF16], tm_gate, "proj_gate",
                                side_casts=[w_ffn_gate[l], w_ffn_up[l]])

    on = _hgrn(q, kk, lf, v, gs, b, s)

    x1, h2 = _mix(z, on, sg, x2d, wc_b, wh_b, wo_b, gt_m, norm_ffn_g[l], sc_f, sh_f, s)
    out = _ffn(h2, wg_b, wu_b, wd_b, x1, gt_f, norm_final_g, s)
    return out.reshape(b, s, d)
```

```python
import functools

import jax
import jax.numpy as jnp
import numpy as np
from jax import lax
from jax.experimental import pallas as pl
from jax.experimental.pallas import tpu as pltpu

F32 = jnp.float32
BF16 = jnp.bfloat16

EPS = 1e-6
CONV_WIDTH = 3
HEADS = 8
HEAD_DIM = 128
CHUNK = 128
LEVELS = 8
FINE = 3
SUBLANES = 8
MIX_PIECE = 256
PROJ_PIECE = 256
VMEM_LIMIT = 56 * 1024 * 1024


def _cparams(*sem):
    return pltpu.CompilerParams(dimension_semantics=sem, vmem_limit_bytes=VMEM_LIMIT)


def _dot(a, b):
    return jnp.dot(a, b, preferred_element_type=F32)


def _dot_nt(a, b):
    return lax.dot_general(a, b, (((1,), (1,)), ((), ())), preferred_element_type=F32)


def _silu(x):
    return x * jax.nn.sigmoid(x)


def _rms(x):
    return x * lax.rsqrt(jnp.mean(x * x, axis=-1, keepdims=True) + EPS)


def _mod_kernel(c_ref, w_ref, b_ref, o_ref):
    c = c_ref[...]
    o_ref[...] = jnp.dot(_silu(c), w_ref[...], preferred_element_type=F32,
                         precision=lax.Precision.HIGHEST) + b_ref[...]


def _modulation(c, w_ada, b_ada):
    b, d = c.shape
    n = w_ada.shape[1]
    tn = 2048
    return pl.pallas_call(
        _mod_kernel,
        grid=(n // tn,),
        in_specs=[pl.BlockSpec((b, d), lambda j: (0, 0)),
                  pl.BlockSpec((d, tn), lambda j: (0, j)),
                  pl.BlockSpec((1, tn), lambda j: (0, j))],
        out_specs=pl.BlockSpec((b, tn), lambda j: (0, j)),
        out_shape=jax.ShapeDtypeStruct((b, n), F32),
        compiler_params=_cparams("parallel"),
        name="mod",
    )(c, w_ada, b_ada.reshape(1, n))


def _rows_lhs(h_ref, rows, extra, outs):
    return h_ref[rows, :]


def _norm_lhs(x_ref, rows, extra, outs):
    g_ref, sc_ref, sh_ref = extra
    y = _rms(x_ref[rows, :]) * g_ref[...]
    h = (y * (1.0 + sc_ref[...]) + sh_ref[...]).astype(BF16)
    outs[-1][rows, :] = h
    return h


def _proj_kernel(*refs, n_w, n_extra, n_side, n_out, epilogue, lhs):
    src_ref = refs[0]
    pos = 1
    w_refs = refs[pos:pos + n_w]
    pos += n_w
    extra = refs[pos:pos + n_extra]
    pos += n_extra
    side_in = refs[pos:pos + n_side]
    pos += n_side
    outs = refs[pos:pos + n_out]
    pos += n_out
    side_out = refs[pos:pos + n_side]
    pos += n_side
    wbf = refs[pos]
    scratch = refs[pos + 1:]

    @pl.when(pl.program_id(1) == 0)
    def _():
        off = 0
        for w_ref in w_refs:
            wd = w_ref.shape[1]
            wbf[:, off:off + wd] = w_ref[...].astype(BF16)
            off += wd

    for s_ref, o_ref in zip(side_in, side_out):
        o_ref[...] = s_ref[...].astype(BF16)

    tm = src_ref.shape[0]
    for r0 in range(0, tm, PROJ_PIECE):
        rows = slice(r0, r0 + PROJ_PIECE)
        epilogue(_dot(lhs(src_ref, rows, extra, outs), wbf[...]), rows, tm, extra, outs, scratch)


def _proj_call(epilogue, h, w, col_starts, wd, nj, extras, extra_specs, out_widths, out_dtypes,
               tm, name, scratch_shapes=(), lhs=_rows_lhs, side_casts=()):
    t, d = h.shape
    ni = t // tm
    in_specs = [pl.BlockSpec((tm, d), lambda j, i: (i, 0))]
    for c0 in col_starts:
        in_specs.append(pl.BlockSpec((d, wd), functools.partial(lambda j, i, b: (0, b + j), b=c0 // wd)))
    in_specs += list(extra_specs)
    out_specs = [pl.BlockSpec((tm, ow), lambda j, i: (i, j)) for ow in out_widths]
    out_shape = [jax.ShapeDtypeStruct((t, nj * ow), dt) for ow, dt in zip(out_widths, out_dtypes)]
    for a in side_casts:
        rb = a.shape[0] // (nj * ni)
        assert rb * nj * ni == a.shape[0] and rb % (2 * SUBLANES) == 0
        spec = pl.BlockSpec((rb, a.shape[1]), lambda j, i: (j * ni + i, 0))
        in_specs.append(spec)
        out_specs.append(spec)
        out_shape.append(jax.ShapeDtypeStruct(a.shape, BF16))
    kern = functools.partial(_proj_kernel, n_w=len(col_starts), n_extra=len(extras),
                             n_side=len(side_casts), n_out=len(out_widths), epilogue=epilogue,
                             lhs=lhs)
    return pl.pallas_call(
        kern,
        grid=(nj, ni),
        in_specs=in_specs,
        out_specs=out_specs,
        out_shape=out_shape,
        scratch_shapes=[pltpu.VMEM((d, wd * len(col_starts)), BF16)] + list(scratch_shapes),
        compiler_params=_cparams("arbitrary", "arbitrary"),
        name=name,
    )(h, *([w] * len(col_starts)), *extras, *side_casts)


def _conv_epilogue(acc, rows, tm, extra, outs, scratch, *, cw, tiles_per_seq):
    cw_ref, = extra
    o_ref, = outs
    ubuf, = scratch
    r0, n = rows.start, rows.stop - rows.start
    u = acc[:, cw:2 * cw] * acc[:, 2 * cw:]

    if r0 == 0:
        @pl.when(pl.program_id(1) % tiles_per_seq == 0)
        def _():
            ubuf[0:SUBLANES, :] = jnp.zeros((SUBLANES, cw), F32)

    ubuf[SUBLANES + r0:SUBLANES + r0 + n, :] = u
    wk = cw_ref[...]
    y = wk[2:3, :] * u
    y = y + wk[1:2, :] * ubuf[SUBLANES - 1 + r0:SUBLANES - 1 + r0 + n, :]
    y = y + wk[0:1, :] * ubuf[SUBLANES - 2 + r0:SUBLANES - 2 + r0 + n, :]
    o_ref[rows, :] = (acc[:, :cw] * y).astype(BF16)
    if rows.stop == tm:
        ubuf[0:SUBLANES, :] = ubuf[tm:tm + SUBLANES, :]


def _silu_epilogue(acc, rows, tm, extra, outs, scratch):
    outs[0][rows, :] = _silu(acc).astype(BF16)


def _gate_gain_epilogue(acc, rows, tm, extra, outs, scratch):
    outs[0][rows, :] = (_silu(acc) * extra[0][...]).astype(BF16)


def _ident_epilogue(acc, rows, tm, extra, outs, scratch):
    outs[0][rows, :] = acc.astype(BF16)


def _sigmoid_epilogue(acc, rows, tm, extra, outs, scratch):
    outs[0][rows, :] = jax.nn.sigmoid(acc).astype(BF16)


def _forget_epilogue(acc, rows, tm, extra, outs, scratch, *, layer):
    k_ref, lf_ref = outs
    p = extra[0][...]
    e = jnp.exp(p - jnp.max(p, axis=0, keepdims=True))
    sm = e / jnp.sum(e, axis=0, keepdims=True)
    lb = jnp.sum(sm[:layer + 1, :], axis=0, keepdims=True)
    f = lb + (1.0 - lb) * jax.nn.sigmoid(acc)
    k_ref[rows, :] = (1.0 - f).astype(BF16)
    lf_ref[rows, :] = jnp.log2(f)


def _level_tables():
    n = CHUNK
    level = -np.ones((n, n), np.int32)
    expo = np.zeros((FINE + 1, n, n), np.float32)
    for t in range(n):
        level[t, t] = 0
        for s in range(t):
            level[t, s] = (t ^ s).bit_length()
        for li in range(1, FINE + 1):
            m = 1 << (li - 1)
            blk = (t // m) * m
            if (t % (2 * m)) >= m:
                expo[li - 1, t, blk:t + 1] = 1.0
            else:
                expo[li - 1, t, t + 1:blk + m] = 1.0
        expo[FINE, t, :t + 1] = 1.0
    expo = expo.reshape((FINE + 1) * n, n)
    return level, np.concatenate([expo, expo], axis=1)


def _hgrn_kernel(q_ref, k_ref, lf_ref, v_ref, gs_ref, lv_ref, ex_ref, o_ref, st_ref, dec_ref):
    c = pl.program_id(1)

    @pl.when(c == 0)
    def _():
        st_ref[...] = jnp.zeros_like(st_ref)

    n = CHUNK
    nb = n // SUBLANES
    g0 = FINE * n
    tile = lambda a, r: a[r * SUBLANES:(r + 1) * SUBLANES, :]

    lf = lf_ref[...]
    hi = lf.astype(BF16)
    lo = (lf - hi.astype(F32)).astype(BF16)
    dec_ref[...] = _dot(ex_ref[...], jnp.concatenate([hi, lo], axis=0))

    def pair_dot_nt(lhs2, rhs2):
        a0, a1 = lhs2
        z = jnp.zeros_like(a0)
        lhs = jnp.concatenate([jnp.concatenate([a0, z], axis=1),
                               jnp.concatenate([z, a1], axis=1)], axis=0)
        p = _dot_nt(lhs, jnp.concatenate(rhs2, axis=1))
        return p[:a0.shape[0]], p[a0.shape[0]:]

    for hd0 in range(0, HEADS, 2):
        pair = (hd0, hd0 + 1)
        sls = [slice(hd * HEAD_DIM, (hd + 1) * HEAD_DIM) for hd in pair]
        qb = [q_ref[:, sl] for sl in sls]
        kb = [k_ref[:, sl] for sl in sls]
        q = [a.astype(F32) for a in qb]
        k = [a.astype(F32) for a in kb]
        g = [dec_ref[g0:g0 + n, sl] for sl in sls]

        def halves_exponent(m, h):
            parts = []
            for r0 in range(0, n, 2 * m):
                g_n = dec_ref[g0 + r0 + m - 1:g0 + r0 + m, sls[h]]
                parts += [g_n - g[h][r0:r0 + m], g[h][r0 + m:r0 + 2 * m] - g_n]
            return jnp.concatenate(parts, axis=0)

        p2 = pair_dot_nt(qb, kb)
        sc = [[jnp.where(tile(lv_ref, r) == 0, tile(p, r), 0.0) for r in range(nb)] for p in p2]
        for li in range(1, LEVELS):
            m = 1 << (li - 1)
            lhs2, rhs2 = [], []
            for h in range(2):
                x = dec_ref[(li - 1) * n:li * n, sls[h]] if li <= FINE else halves_exponent(m, h)
                e = jnp.exp2(x)
                if m < SUBLANES:
                    rows = list(range(nb))
                    lhs2.append((q[h] * e).astype(BF16))
                    rhs2.append((k[h] * e).astype(BF16))
                else:
                    q_parts, k_parts, rows = [], [], []
                    for r0 in range(0, n, 2 * m):
                        k_parts += [k[h][r0:r0 + m] * e[r0:r0 + m], jnp.zeros((m, HEAD_DIM), F32)]
                        q_parts += [q[h][r0 + m:r0 + 2 * m] * e[r0 + m:r0 + 2 * m]]
                        rows += list(range((r0 + m) // SUBLANES, (r0 + 2 * m) // SUBLANES))
                    lhs2.append(jnp.concatenate(q_parts, axis=0).astype(BF16))
                    rhs2.append(jnp.concatenate(k_parts, axis=0).astype(BF16))
            p2 = pair_dot_nt(lhs2, rhs2)
            for h in range(2):
                for i, r in enumerate(rows):
                    sc[h][r] = jnp.where(tile(lv_ref, r) == li, tile(p2[h], i), sc[h][r])

        g_last = [dec_ref[g0 + n - 1:g0 + n, sl] for sl in sls]
        st = [st_ref[hd] for hd in pair]
        o_in = pair_dot_nt([(q[h] * jnp.exp2(g[h])).astype(BF16) for h in range(2)],
                           [a.astype(BF16) for a in st])
        for h in range(2):
            vb = v_ref[:, sls[h]]
            o = o_in[h] + _dot(jnp.concatenate(sc[h], axis=0).astype(BF16), vb)
            k_out = (k[h] * jnp.exp2(g_last[h] - g[h])).astype(BF16)
            vt = vb.astype(F32).T.astype(BF16)
            st_ref[pair[h]] = st[h] * jnp.exp2(g_last[h]) + _dot(vt, k_out)
            o_ref[:, sls[h]] = (_rms(o) * gs_ref[:, sls[h]].astype(F32)).astype(BF16)


def _hgrn(q, k, lf, v, gs, batch, seq):
    t, dk = q.shape
    nc = seq // CHUNK
    level, expo = _level_tables()
    row = lambda b, c: (b * nc + c, 0)
    const = lambda b, c: (0, 0)
    blk = pl.BlockSpec((CHUNK, dk), row)
    return pl.pallas_call(
        _hgrn_kernel,
        grid=(batch, nc),
        in_specs=[blk, blk, blk, blk, blk,
                  pl.BlockSpec((CHUNK, CHUNK), const),
                  pl.BlockSpec(expo.shape, const)],
        out_specs=blk,
        out_shape=jax.ShapeDtypeStruct((t, dk), BF16),
        scratch_shapes=[pltpu.VMEM((HEADS, HEAD_DIM, HEAD_DIM), F32),
                        pltpu.VMEM(((FINE + 1) * CHUNK, dk), F32)],
        compiler_params=_cparams("arbitrary", "arbitrary"),
        name="hgrn",
    )(q, k, lf, v, gs, jnp.asarray(level), jnp.asarray(expo, BF16))


def _mix_kernel(z_ref, on_ref, sg_ref, x_ref, wc_ref, wh_ref, wo_ref, gt_ref, g2_ref, sc_ref,
                sh_ref, x1_ref, h2_ref, *, d):
    for r0 in range(0, z_ref.shape[0], MIX_PIECE):
        rows = slice(r0, r0 + MIX_PIECE)
        y_a = _dot(z_ref[rows, :], wc_ref[...])
        y_b = _dot(on_ref[rows, :], wh_ref[...])
        merged = sg_ref[rows, :d].astype(F32) * y_a + sg_ref[rows, d:].astype(F32) * y_b
        x1 = x_ref[rows, :] + gt_ref[...] * _dot(merged.astype(BF16), wo_ref[...])
        x1_ref[rows, :] = x1
        h2_ref[rows, :] = (_rms(x1) * g2_ref[...] * (1.0 + sc_ref[...]) + sh_ref[...]).astype(BF16)


def _mix(z, on, sg, x, wc, wh, wo, gt, g2, sc, sh, seq):
    t, d = x.shape
    tm = 512
    assert tm % MIX_PIECE == 0
    per = seq // tm
    row = lambda i: (i, 0)
    const = lambda i: (0, 0)
    vec = pl.BlockSpec((None, 1, d), lambda i: (i // per, 0, 0))
    resident = lambda shape: pl.BlockSpec(shape, const, pipeline_mode=pl.Buffered(1))
    return pl.pallas_call(
        functools.partial(_mix_kernel, d=d),
        grid=(t // tm,),
        in_specs=[pl.BlockSpec((tm, z.shape[1]), row),
                  pl.BlockSpec((tm, on.shape[1]), row),
                  pl.BlockSpec((tm, 2 * d), row),
                  pl.BlockSpec((tm, d), row),
                  resident(wc.shape), resident(wh.shape), resident(wo.shape),
                  vec, pl.BlockSpec((1, d), const), vec, vec],
        out_specs=[pl.BlockSpec((tm, d), row), pl.BlockSpec((tm, d), row)],
        out_shape=[jax.ShapeDtypeStruct((t, d), F32), jax.ShapeDtypeStruct((t, d), BF16)],
        compiler_params=_cparams("parallel"),
        name="mix",
    )(z, on, sg, x, wc, wh, wo, gt, g2.reshape(1, d), sc, sh)


def _ffn_kernel(h_ref, wg_ref, wu_ref, wd_ref, x1_hbm, gt_ref, gf_ref, o_ref, x1_buf, x1_sem, *, tm):
    i = pl.program_id(0)
    j = pl.program_id(1)

    def x1_copy():
        return pltpu.make_async_copy(x1_hbm.at[pl.ds(i * tm, tm), :], x1_buf, x1_sem)

    halves = [slice(r0, r0 + tm // 2) for r0 in range(0, tm, tm // 2)]

    def down(rows):
        h = h_ref[rows, :]
        act = (_silu(_dot(h, wg_ref[...])) * _dot(h, wu_ref[...])).astype(BF16)
        return _dot(act, wd_ref[...])

    last = pl.num_programs(1) - 1

    @pl.when(j == 0)
    def _():
        x1_copy().start()
        for rows in halves:
            o_ref[rows, :] = down(rows)

    @pl.when(jnp.logical_and(j > 0, j < last))
    def _():
        for rows in halves:
            o_ref[rows, :] += down(rows)

    @pl.when(j == last)
    def _():
        x1_copy().wait()
        for rows in halves:
            x2 = x1_buf[rows, :] + gt_ref[...] * (o_ref[rows, :] + down(rows))
            o_ref[rows, :] = _rms(x2) * gf_ref[...]


def _ffn(h2, wg, wu, wd, x1, gt, gf, seq):
    t, d = x1.shape
    dff = wg.shape[1]
    tm, tf = 1024, 512
    per = seq // tm
    return pl.pallas_call(
        functools.partial(_ffn_kernel, tm=tm),
        grid=(t // tm, dff // tf),
        in_specs=[pl.BlockSpec((tm, d), lambda i, j: (i, 0)),
                  pl.BlockSpec((d, tf), lambda i, j: (0, j)),
                  pl.BlockSpec((d, tf), lambda i, j: (0, j)),
                  pl.BlockSpec((tf, d), lambda i, j: (j, 0)),
                  pl.BlockSpec(memory_space=pl.ANY),
                  pl.BlockSpec((None, 1, d), lambda i, j: (i // per, 0, 0)),
                  pl.BlockSpec((1, d), lambda i, j: (0, 0))],
        out_specs=pl.BlockSpec((tm, d), lambda i, j: (i, 0)),
        out_shape=jax.ShapeDtypeStruct((t, d), F32),
        scratch_shapes=[pltpu.VMEM((tm, d), F32), pltpu.SemaphoreType.DMA(())],
        compiler_params=_cparams("arbitrary", "arbitrary"),
        name="ffn",
    )(h2, wg, wu, wd, x1, gt, gf.reshape(1, d))


def kernel(x, c, w_ada, b_ada, norm_mix_g, w_in, conv_w, lb_param, gnorm_g, w_conv_out,
           w_hgrn_out, w_o, norm_ffn_g, w_ffn_gate, w_ffn_up, w_ffn_down, norm_final_g):
    b, s, d = x.shape
    t = b * s
    depth = w_in.shape[0]
    dc = conv_w.shape[2]
    dk = lb_param.shape[1]
    dv = w_hgrn_out.shape[1]
    assert dk == HEADS * HEAD_DIM and dv == HEADS * HEAD_DIM and s % CHUNK == 0
    tm = 2048
    tm_conv = 1024
    tm_norm = 1024
    tm_gate = 1024
    cw = 512
    assert dc % cw == 0 and all(s % r == 0 and r % PROJ_PIECE == 0
                                for r in (tm, tm_conv, tm_norm, tm_gate))

    assert depth == 1
    l = 0

    mod = _modulation(c, w_ada[l], b_ada[l]).reshape(b, 6, 1, d)
    sh_m, sc_m, gt_m, sh_f, sc_f, gt_f = [mod[:, i] for i in range(6)]

    w = w_in[l]
    c_q = 3 * dc
    c_f, c_i, c_g, c_gate = c_q + dk, c_q + 2 * dk, c_q + 2 * dk + dv, c_q + 2 * dk + 2 * dv
    tg = 1024
    assert (w.shape[1] - c_gate) % tg == 0 and c_gate % tg == 0 and c_q % dk == 0 and dk == dv

    x2d = x.reshape(t, d)
    vec = pl.BlockSpec((None, 1, d), lambda j, i: (i // (s // tm_norm), 0, 0))
    q, h = _proj_call(
        _silu_epilogue, x2d, w, [c_q], dk, 1, [norm_mix_g[l].reshape(1, d), sc_m, sh_m],
        [pl.BlockSpec((1, d), lambda j, i: (0, 0)), vec, vec], [dk, d], [BF16, BF16], tm_norm,
        "proj_q", lhs=_norm_lhs)
    z, wd_b = _proj_call(
        functools.partial(_conv_epilogue, cw=cw, tiles_per_seq=s // tm_conv),
        h, w, [0, dc, 2 * dc], cw, dc // cw, [conv_w[l]],
        [pl.BlockSpec((CONV_WIDTH, cw), lambda j, i: (0, j))], [cw], [BF16], tm_conv, "proj_conv",
        scratch_shapes=[pltpu.VMEM((tm_conv + SUBLANES, cw), F32)], side_casts=[w_ffn_down[l]])
    kk, lf = _proj_call(
        functools.partial(_forget_epilogue, layer=l), h, w, [c_f], dk, 1, [lb_param],
        [pl.BlockSpec((lb_param.shape[0], dk), lambda j, i: (0, 0))],
        [dk, dk], [BF16, F32], tm, "proj_f")
    v, wc_b, wh_b = _proj_call(_ident_epilogue, h, w, [c_i], dv, 1, [], [], [dv], [BF16], tm,
                               "proj_i", side_casts=[w_conv_out[l], w_hgrn_out[l]])
    gs, wo_b = _proj_call(_gate_gain_epilogue, h, w, [c_g], dv, 1,
                          [jnp.tile(gnorm_g[l], HEADS).reshape(1, dv)],
                          [pl.BlockSpec((1, dv), lambda j, i: (0, 0))], [dv], [BF16], tm, "proj_g",
                          side_casts=[w_o[l]])
    sg, wg_b, wu_b = _proj_call(_sigmoid_epilogue, h, w, [c_gate], tg, (w.shape[1] - c_gate) // tg,
                                [], [], [tg], [BF16], tm_gate, "proj_gate",
                                side_casts=[w_ffn_gate[l], w_ffn_up[l]])

    on = _hgrn(q, kk, lf, v, gs, b, s)

    x1, h2 = _mix(z, on, sg, x2d, wc_b, wh_b, wo_b, gt_m, norm_ffn_g[l], sc_f, sh_f, s)
    out = _ffn(h2, wg_b, wu_b, wd_b, x1, gt_f, norm_final_g, s)
    return out.reshape(b, s, d)
```

```python
import functools

import jax
import jax.numpy as jnp
import numpy as np
from jax import lax
from jax.experimental import pallas as pl
from jax.experimental.pallas import tpu as pltpu

F32 = jnp.float32
BF16 = jnp.bfloat16

EPS = 1e-6
CONV_WIDTH = 3
HEADS = 8
HEAD_DIM = 128
CHUNK = 128
LEVELS = 8
FINE = 3
SUBLANES = 8
MIX_PIECE = 256
PROJ_PIECE = 256
VMEM_LIMIT = 56 * 1024 * 1024


def _cparams(*sem):
    return pltpu.CompilerParams(dimension_semantics=sem, vmem_limit_bytes=VMEM_LIMIT)


def _dot(a, b):
    return jnp.dot(a, b, preferred_element_type=F32)


def _dot_nt(a, b):
    return lax.dot_general(a, b, (((1,), (1,)), ((), ())), preferred_element_type=F32)


def _silu(x):
    return x * jax.nn.sigmoid(x)


def _rms(x):
    return x * lax.rsqrt(jnp.mean(x * x, axis=-1, keepdims=True) + EPS)


def _mod_kernel(c_ref, w_ref, b_ref, o_ref):
    a = _silu(c_ref[...])
    nb = a.shape[0]
    a_hi = a.astype(BF16)
    a2 = jnp.concatenate([a_hi, (a - a_hi.astype(F32)).astype(BF16)], axis=0)
    w = w_ref[...]
    w_hi = w.astype(BF16)
    w_lo = (w - w_hi.astype(F32)).astype(BF16)
    acc = _dot(a2, w_hi) + _dot(a2, w_lo)
    o_ref[...] = acc[:nb] + acc[nb:] + b_ref[...]


def _modulation(c, w_ada, b_ada):
    b, d = c.shape
    n = w_ada.shape[1]
    tn = 1024
    return pl.pallas_call(
        _mod_kernel,
        grid=(n // tn,),
        in_specs=[pl.BlockSpec((b, d), lambda j: (0, 0)),
                  pl.BlockSpec((d, tn), lambda j: (0, j)),
                  pl.BlockSpec((1, tn), lambda j: (0, j))],
        out_specs=pl.BlockSpec((b, tn), lambda j: (0, j)),
        out_shape=jax.ShapeDtypeStruct((b, n), F32),
        compiler_params=_cparams("parallel"),
        name="mod",
    )(c, w_ada, b_ada.reshape(1, n))


def _rows_lhs(h_ref, rows, extra, outs):
    return h_ref[rows, :]


def _norm_lhs(x_ref, rows, extra, outs):
    g_ref, sc_ref, sh_ref = extra
    y = _rms(x_ref[rows, :]) * g_ref[...]
    h = (y * (1.0 + sc_ref[...]) + sh_ref[...]).astype(BF16)
    outs[-1][rows, :] = h
    return h


def _proj_kernel(*refs, n_w, n_extra, n_side, n_out, epilogue, lhs):
    src_ref = refs[0]
    pos = 1
    w_refs = refs[pos:pos + n_w]
    pos += n_w
    extra = refs[pos:pos + n_extra]
    pos += n_extra
    side_in = refs[pos:pos + n_side]
    pos += n_side
    outs = refs[pos:pos + n_out]
    pos += n_out
    side_out = refs[pos:pos + n_side]
    pos += n_side
    wbf = refs[pos]
    scratch = refs[pos + 1:]

    @pl.when(pl.program_id(1) == 0)
    def _():
        off = 0
        for w_ref in w_refs:
            wd = w_ref.shape[1]
            wbf[:, off:off + wd] = w_ref[...].astype(BF16)
            off += wd

    for s_ref, o_ref in zip(side_in, side_out):
        o_ref[...] = s_ref[...].astype(BF16)

    tm = src_ref.shape[0]
    for r0 in range(0, tm, PROJ_PIECE):
        rows = slice(r0, r0 + PROJ_PIECE)
        epilogue(_dot(lhs(src_ref, rows, extra, outs), wbf[...]), rows, tm, extra, outs, scratch)


def _proj_call(epilogue, h, w, col_starts, wd, nj, extras, extra_specs, out_widths, out_dtypes,
               tm, name, scratch_shapes=(), lhs=_rows_lhs, side_casts=()):
    t, d = h.shape
    ni = t // tm
    in_specs = [pl.BlockSpec((tm, d), lambda j, i: (i, 0))]
    for c0 in col_starts:
        in_specs.append(pl.BlockSpec((d, wd), functools.partial(lambda j, i, b: (0, b + j), b=c0 // wd)))
    in_specs += list(extra_specs)
    out_specs = [pl.BlockSpec((tm, ow), lambda j, i: (i, j)) for ow in out_widths]
    out_shape = [jax.ShapeDtypeStruct((t, nj * ow), dt) for ow, dt in zip(out_widths, out_dtypes)]
    for a in side_casts:
        rb = a.shape[0] // (nj * ni)
        assert rb * nj * ni == a.shape[0] and rb % (2 * SUBLANES) == 0
        spec = pl.BlockSpec((rb, a.shape[1]), lambda j, i: (j * ni + i, 0))
        in_specs.append(spec)
        out_specs.append(spec)
        out_shape.append(jax.ShapeDtypeStruct(a.shape, BF16))
    kern = functools.partial(_proj_kernel, n_w=len(col_starts), n_extra=len(extras),
                             n_side=len(side_casts), n_out=len(out_widths), epilogue=epilogue,
                             lhs=lhs)
    return pl.pallas_call(
        kern,
        grid=(nj, ni),
        in_specs=in_specs,
        out_specs=out_specs,
        out_shape=out_shape,
        scratch_shapes=[pltpu.VMEM((d, wd * len(col_starts)), BF16)] + list(scratch_shapes),
        compiler_params=_cparams("arbitrary", "arbitrary"),
        name=name,
    )(h, *([w] * len(col_starts)), *extras, *side_casts)


def _conv_epilogue(acc, rows, tm, extra, outs, scratch, *, cw, tiles_per_seq):
    cw_ref, = extra
    o_ref, = outs
    ubuf, = scratch
    r0, n = rows.start, rows.stop - rows.start
    u = acc[:, cw:2 * cw] * acc[:, 2 * cw:]

    if r0 == 0:
        @pl.when(pl.program_id(1) % tiles_per_seq == 0)
        def _():
            ubuf[0:SUBLANES, :] = jnp.zeros((SUBLANES, cw), F32)

    ubuf[SUBLANES + r0:SUBLANES + r0 + n, :] = u
    wk = cw_ref[...]
    y = wk[2:3, :] * u
    y = y + wk[1:2, :] * ubuf[SUBLANES - 1 + r0:SUBLANES - 1 + r0 + n, :]
    y = y + wk[0:1, :] * ubuf[SUBLANES - 2 + r0:SUBLANES - 2 + r0 + n, :]
    o_ref[rows, :] = (acc[:, :cw] * y).astype(BF16)
    if rows.stop == tm:
        ubuf[0:SUBLANES, :] = ubuf[tm:tm + SUBLANES, :]


def _silu_epilogue(acc, rows, tm, extra, outs, scratch):
    outs[0][rows, :] = _silu(acc).astype(BF16)


def _gate_gain_epilogue(acc, rows, tm, extra, outs, scratch):
    outs[0][rows, :] = (_silu(acc) * extra[0][...]).astype(BF16)


def _ident_epilogue(acc, rows, tm, extra, outs, scratch):
    outs[0][rows, :] = acc.astype(BF16)


def _sigmoid_epilogue(acc, rows, tm, extra, outs, scratch):
    outs[0][rows, :] = jax.nn.sigmoid(acc).astype(BF16)


def _forget_epilogue(acc, rows, tm, extra, outs, scratch, *, layer):
    k_ref, lf_ref = outs
    p = extra[0][...]
    e = jnp.exp(p - jnp.max(p, axis=0, keepdims=True))
    sm = e / jnp.sum(e, axis=0, keepdims=True)
    lb = jnp.sum(sm[:layer + 1, :], axis=0, keepdims=True)
    f = lb + (1.0 - lb) * jax.nn.sigmoid(acc)
    k_ref[rows, :] = (1.0 - f).astype(BF16)
    lf_ref[rows, :] = jnp.log2(f)


def _level_tables():
    n = CHUNK
    level = -np.ones((n, n), np.int32)
    expo = np.zeros((FINE + 1, n, n), np.float32)
    for t in range(n):
        level[t, t] = 0
        for s in range(t):
            level[t, s] = (t ^ s).bit_length()
        for li in range(1, FINE + 1):
            m = 1 << (li - 1)
            blk = (t // m) * m
            if (t % (2 * m)) >= m:
                expo[li - 1, t, blk:t + 1] = 1.0
            else:
                expo[li - 1, t, t + 1:blk + m] = 1.0
        expo[FINE, t, :t + 1] = 1.0
    expo = expo.reshape((FINE + 1) * n, n)
    return level, np.concatenate([expo, expo], axis=1)


def _hgrn_kernel(q_ref, k_ref, lf_ref, v_ref, gs_ref, lv_ref, ex_ref, o_ref, st_ref, dec_ref):
    c = pl.program_id(1)

    @pl.when(c == 0)
    def _():
        st_ref[...] = jnp.zeros_like(st_ref)

    n = CHUNK
    nb = n // SUBLANES
    g0 = FINE * n
    tile = lambda a, r: a[r * SUBLANES:(r + 1) * SUBLANES, :]

    lf = lf_ref[...]
    hi = lf.astype(BF16)
    lo = (lf - hi.astype(F32)).astype(BF16)
    dec_ref[...] = _dot(ex_ref[...], jnp.concatenate([hi, lo], axis=0))

    def pair_dot_nt(lhs2, rhs2):
        a0, a1 = lhs2
        z = jnp.zeros_like(a0)
        lhs = jnp.concatenate([jnp.concatenate([a0, z], axis=1),
                               jnp.concatenate([z, a1], axis=1)], axis=0)
        p = _dot_nt(lhs, jnp.concatenate(rhs2, axis=1))
        return p[:a0.shape[0]], p[a0.shape[0]:]

    for hd0 in range(0, HEADS, 2):
        pair = (hd0, hd0 + 1)
        sls = [slice(hd * HEAD_DIM, (hd + 1) * HEAD_DIM) for hd in pair]
        qb = [q_ref[:, sl] for sl in sls]
        kb = [k_ref[:, sl] for sl in sls]
        q = [a.astype(F32) for a in qb]
        k = [a.astype(F32) for a in kb]
        g = [dec_ref[g0:g0 + n, sl] for sl in sls]

        def halves_exponent(m, h):
            parts = []
            for r0 in range(0, n, 2 * m):
                g_n = dec_ref[g0 + r0 + m - 1:g0 + r0 + m, sls[h]]
                parts += [g_n - g[h][r0:r0 + m], g[h][r0 + m:r0 + 2 * m] - g_n]
            return jnp.concatenate(parts, axis=0)

        p2 = pair_dot_nt(qb, kb)
        sc = [[jnp.where(tile(lv_ref, r) == 0, tile(p, r), 0.0) for r in range(nb)] for p in p2]
        for li in range(1, LEVELS):
            m = 1 << (li - 1)
            lhs2, rhs2 = [], []
            for h in range(2):
                x = dec_ref[(li - 1) * n:li * n, sls[h]] if li <= FINE else halves_exponent(m, h)
                e = jnp.exp2(x)
                if m < SUBLANES:
                    rows = list(range(nb))
                    lhs2.append((q[h] * e).astype(BF16))
                    rhs2.append((k[h] * e).astype(BF16))
                else:
                    q_parts, k_parts, rows = [], [], []
                    for r0 in range(0, n, 2 * m):
                        k_parts += [k[h][r0:r0 + m] * e[r0:r0 + m], jnp.zeros((m, HEAD_DIM), F32)]
                        q_parts += [q[h][r0 + m:r0 + 2 * m] * e[r0 + m:r0 + 2 * m]]
                        rows += list(range((r0 + m) // SUBLANES, (r0 + 2 * m) // SUBLANES))
                    lhs2.append(jnp.concatenate(q_parts, axis=0).astype(BF16))
                    rhs2.append(jnp.concatenate(k_parts, axis=0).astype(BF16))
            p2 = pair_dot_nt(lhs2, rhs2)
            for h in range(2):
                for i, r in enumerate(rows):
                    sc[h][r] = jnp.where(tile(lv_ref, r) == li, tile(p2[h], i), sc[h][r])

        g_last = [dec_ref[g0 + n - 1:g0 + n, sl] for sl in sls]
        st = [st_ref[hd] for hd in pair]
        o_in = pair_dot_nt([(q[h] * jnp.exp2(g[h])).astype(BF16) for h in range(2)],
                           [a.astype(BF16) for a in st])
        for h in range(2):
            vb = v_ref[:, sls[h]]
            o = o_in[h] + _dot(jnp.concatenate(sc[h], axis=0).astype(BF16), vb)
            k_out = (k[h] * jnp.exp2(g_last[h] - g[h])).astype(BF16)
            vt = vb.astype(F32).T.astype(BF16)
            st_ref[pair[h]] = st[h] * jnp.exp2(g_last[h]) + _dot(vt, k_out)
            o_ref[:, sls[h]] = (_rms(o) * gs_ref[:, sls[h]].astype(F32)).astype(BF16)


def _hgrn(q, k, lf, v, gs, batch, seq):
    t, dk = q.shape
    nc = seq // CHUNK
    level, expo = _level_tables()
    row = lambda b, c: (b * nc + c, 0)
    const = lambda b, c: (0, 0)
    blk = pl.BlockSpec((CHUNK, dk), row)
    return pl.pallas_call(
        _hgrn_kernel,
        grid=(batch, nc),
        in_specs=[blk, blk, blk, blk, blk,
                  pl.BlockSpec((CHUNK, CHUNK), const),
                  pl.BlockSpec(expo.shape, const)],
        out_specs=blk,
        out_shape=jax.ShapeDtypeStruct((t, dk), BF16),
        scratch_shapes=[pltpu.VMEM((HEADS, HEAD_DIM, HEAD_DIM), F32),
                        pltpu.VMEM(((FINE + 1) * CHUNK, dk), F32)],
        compiler_params=_cparams("arbitrary", "arbitrary"),
        name="hgrn",
    )(q, k, lf, v, gs, jnp.asarray(level), jnp.asarray(expo, BF16))


def _mix_kernel(z_ref, on_ref, sg_ref, x_ref, wc_ref, wh_ref, wo_ref, gt_ref, g2_ref, sc_ref,
                sh_ref, x1_ref, h2_ref, *, d):
    for r0 in range(0, z_ref.shape[0], MIX_PIECE):
        rows = slice(r0, r0 + MIX_PIECE)
        y_a = _dot(z_ref[rows, :], wc_ref[...])
        y_b = _dot(on_ref[rows, :], wh_ref[...])
        merged = sg_ref[rows, :d].astype(F32) * y_a + sg_ref[rows, d:].astype(F32) * y_b
        x1 = x_ref[rows, :] + gt_ref[...] * _dot(merged.astype(BF16), wo_ref[...])
        x1_ref[rows, :] = x1
        h2_ref[rows, :] = (_rms(x1) * g2_ref[...] * (1.0 + sc_ref[...]) + sh_ref[...]).astype(BF16)


def _mix(z, on, sg, x, wc, wh, wo, gt, g2, sc, sh, seq):
    t, d = x.shape
    tm = 512
    assert tm % MIX_PIECE == 0
    per = seq // tm
    row = lambda i: (i, 0)
    const = lambda i: (0, 0)
    vec = pl.BlockSpec((None, 1, d), lambda i: (i // per, 0, 0))
    resident = lambda shape: pl.BlockSpec(shape, const, pipeline_mode=pl.Buffered(1))
    return pl.pallas_call(
        functools.partial(_mix_kernel, d=d),
        grid=(t // tm,),
        in_specs=[pl.BlockSpec((tm, z.shape[1]), row),
                  pl.BlockSpec((tm, on.shape[1]), row),
                  pl.BlockSpec((tm, 2 * d), row),
                  pl.BlockSpec((tm, d), row),
                  resident(wc.shape), resident(wh.shape), resident(wo.shape),
                  vec, pl.BlockSpec((1, d), const), vec, vec],
        out_specs=[pl.BlockSpec((tm, d), row), pl.BlockSpec((tm, d), row)],
        out_shape=[jax.ShapeDtypeStruct((t, d), F32), jax.ShapeDtypeStruct((t, d), BF16)],
        compiler_params=_cparams("parallel"),
        name="mix",
    )(z, on, sg, x, wc, wh, wo, gt, g2.reshape(1, d), sc, sh)


def _ffn_kernel(h_ref, wg_ref, wu_ref, wd_ref, x1_hbm, gt_ref, gf_ref, o_ref, x1_buf, x1_sem, *, tm):
    i = pl.program_id(0)
    j = pl.program_id(1)

    def x1_copy():
        return pltpu.make_async_copy(x1_hbm.at[pl.ds(i * tm, tm), :], x1_buf, x1_sem)

    halves = [slice(r0, r0 + tm // 2) for r0 in range(0, tm, tm // 2)]

    def down(rows):
        h = h_ref[rows, :]
        act = (_silu(_dot(h, wg_ref[...])) * _dot(h, wu_ref[...])).astype(BF16)
        return _dot(act, wd_ref[...])

    last = pl.num_programs(1) - 1

    @pl.when(j == 0)
    def _():
        x1_copy().start()
        for rows in halves:
            o_ref[rows, :] = down(rows)

    @pl.when(jnp.logical_and(j > 0, j < last))
    def _():
        for rows in halves:
            o_ref[rows, :] += down(rows)

    @pl.when(j == last)
    def _():
        x1_copy().wait()
        for rows in halves:
            x2 = x1_buf[rows, :] + gt_ref[...] * (o_ref[rows, :] + down(rows))
            o_ref[rows, :] = _rms(x2) * gf_ref[...]


def _ffn(h2, wg, wu, wd, x1, gt, gf, seq):
    t, d = x1.shape
    dff = wg.shape[1]
    tm, tf = 1024, 512
    per = seq // tm
    return pl.pallas_call(
        functools.partial(_ffn_kernel, tm=tm),
        grid=(t // tm, dff // tf),
        in_specs=[pl.BlockSpec((tm, d), lambda i, j: (i, 0)),
                  pl.BlockSpec((d, tf), lambda i, j: (0, j)),
                  pl.BlockSpec((d, tf), lambda i, j: (0, j)),
                  pl.BlockSpec((tf, d), lambda i, j: (j, 0)),
                  pl.BlockSpec(memory_space=pl.ANY),
                  pl.BlockSpec((None, 1, d), lambda i, j: (i // per, 0, 0)),
                  pl.BlockSpec((1, d), lambda i, j: (0, 0))],
        out_specs=pl.BlockSpec((tm, d), lambda i, j: (i, 0)),
        out_shape=jax.ShapeDtypeStruct((t, d), F32),
        scratch_shapes=[pltpu.VMEM((tm, d), F32), pltpu.SemaphoreType.DMA(())],
        compiler_params=_cparams("arbitrary", "arbitrary"),
        name="ffn",
    )(h2, wg, wu, wd, x1, gt, gf.reshape(1, d))


def kernel(x, c, w_ada, b_ada, norm_mix_g, w_in, conv_w, lb_param, gnorm_g, w_conv_out,
           w_hgrn_out, w_o, norm_ffn_g, w_ffn_gate, w_ffn_up, w_ffn_down, norm_final_g):
    b, s, d = x.shape
    t = b * s
    depth = w_in.shape[0]
    dc = conv_w.shape[2]
    dk = lb_param.shape[1]
    dv = w_hgrn_out.shape[1]
    assert dk == HEADS * HEAD_DIM and dv == HEADS * HEAD_DIM and s % CHUNK == 0
    tm = 2048
    tm_conv = 1024
    tm_norm = 1024
    tm_gate = 1024
    cw = 512
    assert dc % cw == 0 and all(s % r == 0 and r % PROJ_PIECE == 0
                                for r in (tm, tm_conv, tm_norm, tm_gate))

    assert depth == 1
    l = 0

    mod = _modulation(c, w_ada[l], b_ada[l]).reshape(b, 6, 1, d)
    sh_m, sc_m, gt_m, sh_f, sc_f, gt_f = [mod[:, i] for i in range(6)]

    w = w_in[l]
    c_q = 3 * dc
    c_f, c_i, c_g, c_gate = c_q + dk, c_q + 2 * dk, c_q + 2 * dk + dv, c_q + 2 * dk + 2 * dv
    tg = 1024
    assert (w.shape[1] - c_gate) % tg == 0 and c_gate % tg == 0 and c_q % dk == 0 and dk == dv

    x2d = x.reshape(t, d)
    vec = pl.BlockSpec((None, 1, d), lambda j, i: (i // (s // tm_norm), 0, 0))
    q, h = _proj_call(
        _silu_epilogue, x2d, w, [c_q], dk, 1, [norm_mix_g[l].reshape(1, d), sc_m, sh_m],
        [pl.BlockSpec((1, d), lambda j, i: (0, 0)), vec, vec], [dk, d], [BF16, BF16], tm_norm,
        "proj_q", lhs=_norm_lhs)
    z, wd_b = _proj_call(
        functools.partial(_conv_epilogue, cw=cw, tiles_per_seq=s // tm_conv),
        h, w, [0, dc, 2 * dc], cw, dc // cw, [conv_w[l]],
        [pl.BlockSpec((CONV_WIDTH, cw), lambda j, i: (0, j))], [cw], [BF16], tm_conv, "proj_conv",
        scratch_shapes=[pltpu.VMEM((tm_conv + SUBLANES, cw), F32)], side_casts=[w_ffn_down[l]])
    kk, lf = _proj_call(
        functools.partial(_forget_epilogue, layer=l), h, w, [c_f], dk, 1, [lb_param],
        [pl.BlockSpec((lb_param.shape[0], dk), lambda j, i: (0, 0))],
        [dk, dk], [BF16, F32], tm, "proj_f")
    v, wc_b, wh_b = _proj_call(_ident_epilogue, h, w, [c_i], dv, 1, [], [], [dv], [BF16], tm,
                               "proj_i", side_casts=[w_conv_out[l], w_hgrn_out[l]])
    gs, wo_b = _proj_call(_gate_gain_epilogue, h, w, [c_g], dv, 1,
                          [jnp.tile(gnorm_g[l], HEADS).reshape(1, dv)],
                          [pl.BlockSpec((1, dv), lambda j, i: (0, 0))], [dv], [BF16], tm, "proj_g",
                          side_casts=[w_o[l]])
    sg, wg_b, wu_b = _proj_call(_sigmoid_epilogue, h, w, [c_gate], tg, (w.shape[1] - c_gate) // tg,
                                [], [], [tg], [BF16], tm_gate, "proj_gate",
                                side_casts=[w_ffn_gate[l], w_ffn_up[l]])

    on = _hgrn(q, kk, lf, v, gs, b, s)

    x1, h2 = _mix(z, on, sg, x2d, wc_b, wh_b, wo_b, gt_m, norm_ffn_g[l], sc_f, sh_f, s)
    out = _ffn(h2, wg_b, wu_b, wd_b, x1, gt_f, norm_final_g, s)
    return out.reshape(b, s, d)
```

```python
import functools

import jax
import jax.numpy as jnp
import numpy as np
from jax import lax
from jax.experimental import pallas as pl
from jax.experimental.pallas import tpu as pltpu

F32 = jnp.float32
BF16 = jnp.bfloat16

EPS = 1e-6
CONV_WIDTH = 3
HEADS = 8
HEAD_DIM = 128
CHUNK = 128
LEVELS = 8
FINE = 3
SUBLANES = 8
GATE_PIECES = 2
MIX_PIECE = 256
PROJ_PIECE = 256
VMEM_LIMIT = 56 * 1024 * 1024


def _cparams(*sem):
    return pltpu.CompilerParams(dimension_semantics=sem, vmem_limit_bytes=VMEM_LIMIT)


def _dot(a, b):
    return jnp.dot(a, b, preferred_element_type=F32)


def _dot_nt(a, b):
    return lax.dot_general(a, b, (((1,), (1,)), ((), ())), preferred_element_type=F32)


def _silu(x):
    return x * jax.nn.sigmoid(x)


def _rms(x):
    return x * lax.rsqrt(jnp.mean(x * x, axis=-1, keepdims=True) + EPS)


def _mod_kernel(c_ref, w_ref, b_ref, o_ref):
    a = _silu(c_ref[...])
    nb = a.shape[0]
    a_hi = a.astype(BF16)
    a2 = jnp.concatenate([a_hi, (a - a_hi.astype(F32)).astype(BF16)], axis=0)
    w = w_ref[...]
    w_hi = w.astype(BF16)
    w_lo = (w - w_hi.astype(F32)).astype(BF16)
    acc = _dot(a2, w_hi) + _dot(a2, w_lo)
    o_ref[...] = acc[:nb] + acc[nb:] + b_ref[...]


def _modulation(c, w_ada, b_ada):
    b, d = c.shape
    n = w_ada.shape[1]
    tn = 1024
    return pl.pallas_call(
        _mod_kernel,
        grid=(n // tn,),
        in_specs=[pl.BlockSpec((b, d), lambda j: (0, 0)),
                  pl.BlockSpec((d, tn), lambda j: (0, j)),
                  pl.BlockSpec((1, tn), lambda j: (0, j))],
        out_specs=pl.BlockSpec((b, tn), lambda j: (0, j)),
        out_shape=jax.ShapeDtypeStruct((b, n), F32),
        compiler_params=_cparams("parallel"),
        name="mod",
    )(c, w_ada, b_ada.reshape(1, n))


def _rows_lhs(h_ref, rows, extra, outs):
    return h_ref[rows, :]


def _norm_lhs(x_ref, rows, extra, outs):
    g_ref, sc_ref, sh_ref = extra
    y = _rms(x_ref[rows, :]) * g_ref[...]
    h = (y * (1.0 + sc_ref[...]) + sh_ref[...]).astype(BF16)
    outs[-1][rows, :] = h
    return h


def _proj_kernel(*refs, n_w, n_extra, n_side, n_out, epilogue, lhs):
    src_ref = refs[0]
    pos = 1
    w_refs = refs[pos:pos + n_w]
    pos += n_w
    extra = refs[pos:pos + n_extra]
    pos += n_extra
    side_in = refs[pos:pos + n_side]
    pos += n_side
    outs = refs[pos:pos + n_out]
    pos += n_out
    side_out = refs[pos:pos + n_side]
    pos += n_side
    wbf = refs[pos]
    scratch = refs[pos + 1:]

    @pl.when(pl.program_id(1) == 0)
    def _():
        off = 0
        for w_ref in w_refs:
            wd = w_ref.shape[1]
            wbf[:, off:off + wd] = w_ref[...].astype(BF16)
            off += wd

    for s_ref, o_ref in zip(side_in, side_out):
        o_ref[...] = s_ref[...].astype(BF16)

    tm = src_ref.shape[0]
    for r0 in range(0, tm, PROJ_PIECE):
        rows = slice(r0, r0 + PROJ_PIECE)
        epilogue(_dot(lhs(src_ref, rows, extra, outs), wbf[...]), rows, tm, extra, outs, scratch)


def _proj_call(epilogue, h, w, col_starts, wd, nj, extras, extra_specs, out_widths, out_dtypes,
               tm, name, scratch_shapes=(), lhs=_rows_lhs, side_casts=()):
    t, d = h.shape
    ni = t // tm
    in_specs = [pl.BlockSpec((tm, d), lambda j, i: (i, 0))]
    for c0 in col_starts:
        in_specs.append(pl.BlockSpec((d, wd), functools.partial(lambda j, i, b: (0, b + j), b=c0 // wd)))
    in_specs += list(extra_specs)
    out_specs = [pl.BlockSpec((tm, ow), lambda j, i: (i, j)) for ow in out_widths]
    out_shape = [jax.ShapeDtypeStruct((t, nj * ow), dt) for ow, dt in zip(out_widths, out_dtypes)]
    for a in side_casts:
        rb = a.shape[0] // (nj * ni)
        assert rb * nj * ni == a.shape[0] and rb % (2 * SUBLANES) == 0
        spec = pl.BlockSpec((rb, a.shape[1]), lambda j, i: (j * ni + i, 0))
        in_specs.append(spec)
        out_specs.append(spec)
        out_shape.append(jax.ShapeDtypeStruct(a.shape, BF16))
    kern = functools.partial(_proj_kernel, n_w=len(col_starts), n_extra=len(extras),
                             n_side=len(side_casts), n_out=len(out_widths), epilogue=epilogue,
                             lhs=lhs)
    return pl.pallas_call(
        kern,
        grid=(nj, ni),
        in_specs=in_specs,
        out_specs=out_specs,
        out_shape=out_shape,
        scratch_shapes=[pltpu.VMEM((d, wd * len(col_starts)), BF16)] + list(scratch_shapes),
        compiler_params=_cparams("arbitrary", "arbitrary"),
        name=name,
    )(h, *([w] * len(col_starts)), *extras, *side_casts)


def _conv_epilogue(acc, rows, tm, extra, outs, scratch, *, cw, tiles_per_seq):
    cw_ref, = extra
    o_ref, = outs
    ubuf, = scratch
    r0, n = rows.start, rows.stop - rows.start
    u = acc[:, cw:2 * cw] * acc[:, 2 * cw:]

    if r0 == 0:
        @pl.when(pl.program_id(1) % tiles_per_seq == 0)
        def _():
            ubuf[0:SUBLANES, :] = jnp.zeros((SUBLANES, cw), F32)

    ubuf[SUBLANES + r0:SUBLANES + r0 + n, :] = u
    wk = cw_ref[...]
    y = wk[2:3, :] * u
    y = y + wk[1:2, :] * ubuf[SUBLANES - 1 + r0:SUBLANES - 1 + r0 + n, :]
    y = y + wk[0:1, :] * ubuf[SUBLANES - 2 + r0:SUBLANES - 2 + r0 + n, :]
    o_ref[rows, :] = (acc[:, :cw] * y).astype(BF16)
    if rows.stop == tm:
        ubuf[0:SUBLANES, :] = ubuf[tm:tm + SUBLANES, :]


def _silu_epilogue(acc, rows, tm, extra, outs, scratch):
    outs[0][rows, :] = _silu(acc).astype(BF16)


def _gate_gain_epilogue(acc, rows, tm, extra, outs, scratch):
    outs[0][rows, :] = (_silu(acc) * extra[0][...]).astype(BF16)


def _ident_epilogue(acc, rows, tm, extra, outs, scratch):
    outs[0][rows, :] = acc.astype(BF16)


def _forget_epilogue(acc, rows, tm, extra, outs, scratch, *, layer):
    k_ref, lf_ref = outs
    p = extra[0][...]
    e = jnp.exp(p - jnp.max(p, axis=0, keepdims=True))
    sm = e / jnp.sum(e, axis=0, keepdims=True)
    lb = jnp.sum(sm[:layer + 1, :], axis=0, keepdims=True)
    f = lb + (1.0 - lb) * jax.nn.sigmoid(acc)
    k_ref[rows, :] = (1.0 - f).astype(BF16)
    lf_ref[rows, :] = jnp.log2(f)


def _level_tables():
    n = CHUNK
    level = -np.ones((n, n), np.int32)
    expo = np.zeros((FINE + 1, n, n), np.float32)
    for t in range(n):
        level[t, t] = 0
        for s in range(t):
            level[t, s] = (t ^ s).bit_length()
        for li in range(1, FINE + 1):
            m = 1 << (li - 1)
            blk = (t // m) * m
            if (t % (2 * m)) >= m:
                expo[li - 1, t, blk:t + 1] = 1.0
            else:
                expo[li - 1, t, t + 1:blk + m] = 1.0
        expo[FINE, t, :t + 1] = 1.0
    expo = expo.reshape((FINE + 1) * n, n)
    return level, np.concatenate([expo, expo], axis=1)


def _hgrn_gate_kernel(q_ref, k_ref, lf_ref, v_ref, gs_ref, lv_ref, ex_ref, h_ref, w_ref, *rest,
                      n_side):
    side_in = rest[:n_side]
    o_ref, sg_ref = rest[n_side:n_side + 2]
    side_out = rest[n_side + 2:2 * n_side + 2]
    st_ref, dec_ref, wbf = rest[2 * n_side + 2:]

    @pl.when(pl.program_id(1) == 0)
    def _():
        st_ref[...] = jnp.zeros_like(st_ref)
        wbf[...] = w_ref[...].astype(BF16)

    for s_ref, so_ref in zip(side_in, side_out):
        so_ref[...] = s_ref[...].astype(BF16)

    piece = h_ref.shape[0] // GATE_PIECES

    def gate_piece(p):
        if p % (HEADS // 2 // GATE_PIECES) == 0:
            p //= HEADS // 2 // GATE_PIECES
            rows = slice(p * piece, (p + 1) * piece)
            sg_ref[rows, :] = jax.nn.sigmoid(_dot(h_ref[rows, :], wbf[...])).astype(BF16)

    _hgrn_step(q_ref, k_ref, lf_ref, v_ref, gs_ref, lv_ref, ex_ref, o_ref, st_ref, dec_ref,
               before_pair=gate_piece)


def _hgrn_step(q_ref, k_ref, lf_ref, v_ref, gs_ref, lv_ref, ex_ref, o_ref, st_ref, dec_ref,
               before_pair):
    n = CHUNK
    nb = n // SUBLANES
    g0 = FINE * n
    tile = lambda a, r: a[r * SUBLANES:(r + 1) * SUBLANES, :]

    lf = lf_ref[...]
    hi = lf.astype(BF16)
    lo = (lf - hi.astype(F32)).astype(BF16)
    dec_ref[...] = _dot(ex_ref[...], jnp.concatenate([hi, lo], axis=0))

    def pair_dot_nt(lhs2, rhs2):
        a0, a1 = lhs2
        z = jnp.zeros_like(a0)
        lhs = jnp.concatenate([jnp.concatenate([a0, z], axis=1),
                               jnp.concatenate([z, a1], axis=1)], axis=0)
        p = _dot_nt(lhs, jnp.concatenate(rhs2, axis=1))
        return p[:a0.shape[0]], p[a0.shape[0]:]

    for hd0 in range(0, HEADS, 2):
        before_pair(hd0 // 2)
        pair = (hd0, hd0 + 1)
        sls = [slice(hd * HEAD_DIM, (hd + 1) * HEAD_DIM) for hd in pair]
        qb = [q_ref[:, sl] for sl in sls]
        kb = [k_ref[:, sl] for sl in sls]
        q = [a.astype(F32) for a in qb]
        k = [a.astype(F32) for a in kb]
        g = [dec_ref[g0:g0 + n, sl] for sl in sls]

        def halves_exponent(m, h):
            parts = []
            for r0 in range(0, n, 2 * m):
                g_n = dec_ref[g0 + r0 + m - 1:g0 + r0 + m, sls[h]]
                parts += [g_n - g[h][r0:r0 + m], g[h][r0 + m:r0 + 2 * m] - g_n]
            return jnp.concatenate(parts, axis=0)

        p2 = pair_dot_nt(qb, kb)
        sc = [[jnp.where(tile(lv_ref, r) == 0, tile(p, r), 0.0) for r in range(nb)] for p in p2]
        for li in range(1, LEVELS):
            m = 1 << (li - 1)
            lhs2, rhs2 = [], []
            for h in range(2):
                x = dec_ref[(li - 1) * n:li * n, sls[h]] if li <= FINE else halves_exponent(m, h)
                e = jnp.exp2(x)
                if m < SUBLANES:
                    rows = list(range(nb))
                    lhs2.append((q[h] * e).astype(BF16))
                    rhs2.append((k[h] * e).astype(BF16))
                else:
                    q_parts, k_parts, rows = [], [], []
                    for r0 in range(0, n, 2 * m):
                        k_parts += [k[h][r0:r0 + m] * e[r0:r0 + m], jnp.zeros((m, HEAD_DIM), F32)]
                        q_parts += [q[h][r0 + m:r0 + 2 * m] * e[r0 + m:r0 + 2 * m]]
                        rows += list(range((r0 + m) // SUBLANES, (r0 + 2 * m) // SUBLANES))
                    lhs2.append(jnp.concatenate(q_parts, axis=0).astype(BF16))
                    rhs2.append(jnp.concatenate(k_parts, axis=0).astype(BF16))
            p2 = pair_dot_nt(lhs2, rhs2)
            for h in range(2):
                for i, r in enumerate(rows):
                    sc[h][r] = jnp.where(tile(lv_ref, r) == li, tile(p2[h], i), sc[h][r])

        g_last = [dec_ref[g0 + n - 1:g0 + n, sl] for sl in sls]
        st = [st_ref[hd] for hd in pair]
        o_in = pair_dot_nt([(q[h] * jnp.exp2(g[h])).astype(BF16) for h in range(2)],
                           [a.astype(BF16) for a in st])
        for h in range(2):
            vb = v_ref[:, sls[h]]
            o = o_in[h] + _dot(jnp.concatenate(sc[h], axis=0).astype(BF16), vb)
            k_out = (k[h] * jnp.exp2(g_last[h] - g[h])).astype(BF16)
            vt = vb.astype(F32).T.astype(BF16)
            st_ref[pair[h]] = st[h] * jnp.exp2(g_last[h]) + _dot(vt, k_out)
            o_ref[:, sls[h]] = (_rms(o) * gs_ref[:, sls[h]].astype(F32)).astype(BF16)


def _hgrn_gate(q, k, lf, v, gs, h, w, c_gate, side_casts, batch, seq):
    t, dk = q.shape
    d = h.shape[1]
    nc = seq // CHUNK
    tm = t // nc
    tg = (w.shape[1] - c_gate) // batch
    assert tm % PROJ_PIECE == 0 and tg * batch == w.shape[1] - c_gate and c_gate % tg == 0
    level, expo = _level_tables()
    row = lambda b, c: (b * nc + c, 0)
    const = lambda b, c: (0, 0)
    blk = pl.BlockSpec((CHUNK, dk), row)
    in_specs = [blk, blk, blk, blk, blk,
                pl.BlockSpec((CHUNK, CHUNK), const),
                pl.BlockSpec(expo.shape, const),
                pl.BlockSpec((tm, d), lambda j, i: (i, 0)),
                pl.BlockSpec((d, tg), lambda j, i: (0, c_gate // tg + j))]
    out_specs = [blk, pl.BlockSpec((tm, tg), lambda j, i: (i, j))]
    out_shape = [jax.ShapeDtypeStruct((t, dk), BF16), jax.ShapeDtypeStruct((t, batch * tg), BF16)]
    for a in side_casts:
        rb = a.shape[0] // (batch * nc)
        assert rb * batch * nc == a.shape[0] and rb % (2 * SUBLANES) == 0
        spec = pl.BlockSpec((rb, a.shape[1]), lambda j, i: (j * nc + i, 0))
        in_specs.append(spec)
        out_specs.append(spec)
        out_shape.append(jax.ShapeDtypeStruct(a.shape, BF16))
    return pl.pallas_call(
        functools.partial(_hgrn_gate_kernel, n_side=len(side_casts)),
        grid=(batch, nc),
        in_specs=in_specs,
        out_specs=out_specs,
        out_shape=out_shape,
        scratch_shapes=[pltpu.VMEM((HEADS, HEAD_DIM, HEAD_DIM), F32),
                        pltpu.VMEM(((FINE + 1) * CHUNK, dk), F32),
                        pltpu.VMEM((d, tg), BF16)],
        compiler_params=_cparams("arbitrary", "arbitrary"),
        name="hgrn_gate",
    )(q, k, lf, v, gs, jnp.asarray(level), jnp.asarray(expo, BF16), h, w, *side_casts)


def _mix_kernel(z_ref, on_ref, sg_ref, x_ref, wc_ref, wh_ref, wo_ref, gt_ref, g2_ref, sc_ref,
                sh_ref, x1_ref, h2_ref, *, d):
    for r0 in range(0, z_ref.shape[0], MIX_PIECE):
        rows = slice(r0, r0 + MIX_PIECE)
        y_a = _dot(z_ref[rows, :], wc_ref[...])
        y_b = _dot(on_ref[rows, :], wh_ref[...])
        merged = sg_ref[rows, :d].astype(F32) * y_a + sg_ref[rows, d:].astype(F32) * y_b
        x1 = x_ref[rows, :] + gt_ref[...] * _dot(merged.astype(BF16), wo_ref[...])
        x1_ref[rows, :] = x1
        h2_ref[rows, :] = (_rms(x1) * g2_ref[...] * (1.0 + sc_ref[...]) + sh_ref[...]).astype(BF16)


def _mix(z, on, sg, x, wc, wh, wo, gt, g2, sc, sh, seq):
    t, d = x.shape
    tm = 512
    assert tm % MIX_PIECE == 0
    per = seq // tm
    row = lambda i: (i, 0)
    const = lambda i: (0, 0)
    vec = pl.BlockSpec((None, 1, d), lambda i: (i // per, 0, 0))
    resident = lambda shape: pl.BlockSpec(shape, const, pipeline_mode=pl.Buffered(1))
    return pl.pallas_call(
        functools.partial(_mix_kernel, d=d),
        grid=(t // tm,),
        in_specs=[pl.BlockSpec((tm, z.shape[1]), row),
                  pl.BlockSpec((tm, on.shape[1]), row),
                  pl.BlockSpec((tm, 2 * d), row),
                  pl.BlockSpec((tm, d), row),
                  resident(wc.shape), resident(wh.shape), resident(wo.shape),
                  vec, pl.BlockSpec((1, d), const), vec, vec],
        out_specs=[pl.BlockSpec((tm, d), row), pl.BlockSpec((tm, d), row)],
        out_shape=[jax.ShapeDtypeStruct((t, d), F32), jax.ShapeDtypeStruct((t, d), BF16)],
        compiler_params=_cparams("parallel"),
        name="mix",
    )(z, on, sg, x, wc, wh, wo, gt, g2.reshape(1, d), sc, sh)


def _ffn_kernel(h_ref, wg_ref, wu_ref, wd_ref, x1_hbm, gt_ref, gf_ref, o_ref, x1_buf, x1_sem, *, tm):
    i = pl.program_id(0)
    j = pl.program_id(1)

    def x1_copy():
        return pltpu.make_async_copy(x1_hbm.at[pl.ds(i * tm, tm), :], x1_buf, x1_sem)

    halves = [slice(r0, r0 + tm // 2) for r0 in range(0, tm, tm // 2)]

    def down(rows):
        h = h_ref[rows, :]
        act = (_silu(_dot(h, wg_ref[...])) * _dot(h, wu_ref[...])).astype(BF16)
        return _dot(act, wd_ref[...])

    last = pl.num_programs(1) - 1

    @pl.when(j == 0)
    def _():
        x1_copy().start()
        for rows in halves:
            o_ref[rows, :] = down(rows)

    @pl.when(jnp.logical_and(j > 0, j < last))
    def _():
        for rows in halves:
            o_ref[rows, :] += down(rows)

    @pl.when(j == last)
    def _():
        x1_copy().wait()
        for rows in halves:
            x2 = x1_buf[rows, :] + gt_ref[...] * (o_ref[rows, :] + down(rows))
            o_ref[rows, :] = _rms(x2) * gf_ref[...]


def _ffn(h2, wg, wu, wd, x1, gt, gf, seq):
    t, d = x1.shape
    dff = wg.shape[1]
    tm, tf = 1024, 512
    per = seq // tm
    return pl.pallas_call(
        functools.partial(_ffn_kernel, tm=tm),
        grid=(t // tm, dff // tf),
        in_specs=[pl.BlockSpec((tm, d), lambda i, j: (i, 0)),
                  pl.BlockSpec((d, tf), lambda i, j: (0, j)),
                  pl.BlockSpec((d, tf), lambda i, j: (0, j)),
                  pl.BlockSpec((tf, d), lambda i, j: (j, 0)),
                  pl.BlockSpec(memory_space=pl.ANY),
                  pl.BlockSpec((None, 1, d), lambda i, j: (i // per, 0, 0)),
                  pl.BlockSpec((1, d), lambda i, j: (0, 0))],
        out_specs=pl.BlockSpec((tm, d), lambda i, j: (i, 0)),
        out_shape=jax.ShapeDtypeStruct((t, d), F32),
        scratch_shapes=[pltpu.VMEM((tm, d), F32), pltpu.SemaphoreType.DMA(())],
        compiler_params=_cparams("arbitrary", "arbitrary"),
        name="ffn",
    )(h2, wg, wu, wd, x1, gt, gf.reshape(1, d))


def kernel(x, c, w_ada, b_ada, norm_mix_g, w_in, conv_w, lb_param, gnorm_g, w_conv_out,
           w_hgrn_out, w_o, norm_ffn_g, w_ffn_gate, w_ffn_up, w_ffn_down, norm_final_g):
    b, s, d = x.shape
    t = b * s
    depth = w_in.shape[0]
    dc = conv_w.shape[2]
    dk = lb_param.shape[1]
    dv = w_hgrn_out.shape[1]
    assert dk == HEADS * HEAD_DIM and dv == HEADS * HEAD_DIM and s % CHUNK == 0
    tm = 2048
    tm_conv = 1024
    tm_norm = 1024
    cw = 512
    assert dc % cw == 0 and all(s % r == 0 and r % PROJ_PIECE == 0
                                for r in (tm, tm_conv, tm_norm))

    assert depth == 1
    l = 0

    mod = _modulation(c, w_ada[l], b_ada[l]).reshape(b, 6, 1, d)
    sh_m, sc_m, gt_m, sh_f, sc_f, gt_f = [mod[:, i] for i in range(6)]

    w = w_in[l]
    c_q = 3 * dc
    c_f, c_i, c_g, c_gate = c_q + dk, c_q + 2 * dk, c_q + 2 * dk + dv, c_q + 2 * dk + 2 * dv
    assert c_q % dk == 0 and dk == dv

    x2d = x.reshape(t, d)
    vec = pl.BlockSpec((None, 1, d), lambda j, i: (i // (s // tm_norm), 0, 0))
    q, h = _proj_call(
        _silu_epilogue, x2d, w, [c_q], dk, 1, [norm_mix_g[l].reshape(1, d), sc_m, sh_m],
        [pl.BlockSpec((1, d), lambda j, i: (0, 0)), vec, vec], [dk, d], [BF16, BF16], tm_norm,
        "proj_q", lhs=_norm_lhs)
    z, wd_b = _proj_call(
        functools.partial(_conv_epilogue, cw=cw, tiles_per_seq=s // tm_conv),
        h, w, [0, dc, 2 * dc], cw, dc // cw, [conv_w[l]],
        [pl.BlockSpec((CONV_WIDTH, cw), lambda j, i: (0, j))], [cw], [BF16], tm_conv, "proj_conv",
        scratch_shapes=[pltpu.VMEM((tm_conv + SUBLANES, cw), F32)], side_casts=[w_ffn_down[l]])
    kk, lf = _proj_call(
        functools.partial(_forget_epilogue, layer=l), h, w, [c_f], dk, 1, [lb_param],
        [pl.BlockSpec((lb_param.shape[0], dk), lambda j, i: (0, 0))],
        [dk, dk], [BF16, F32], tm, "proj_f")
    v, wc_b, wh_b = _proj_call(_ident_epilogue, h, w, [c_i], dv, 1, [], [], [dv], [BF16], tm,
                               "proj_i", side_casts=[w_conv_out[l], w_hgrn_out[l]])
    gs, wo_b = _proj_call(_gate_gain_epilogue, h, w, [c_g], dv, 1,
                          [jnp.tile(gnorm_g[l], HEADS).reshape(1, dv)],
                          [pl.BlockSpec((1, dv), lambda j, i: (0, 0))], [dv], [BF16], tm, "proj_g",
                          side_casts=[w_o[l]])
    on, sg, wg_b, wu_b = _hgrn_gate(q, kk, lf, v, gs, h, w, c_gate,
                                    [w_ffn_gate[l], w_ffn_up[l]], b, s)

    x1, h2 = _mix(z, on, sg, x2d, wc_b, wh_b, wo_b, gt_m, norm_ffn_g[l], sc_f, sh_f, s)
    out = _ffn(h2, wg_b, wu_b, wd_b, x1, gt_f, norm_final_g, s)
    return out.reshape(b, s, d)
```

```python
import functools

import jax
import jax.numpy as jnp
import numpy as np
from jax import lax
from jax.experimental import pallas as pl
from jax.experimental.pallas import tpu as pltpu

F32 = jnp.float32
BF16 = jnp.bfloat16

EPS = 1e-6
CONV_WIDTH = 3
HEADS = 8
HEAD_DIM = 128
CHUNK = 128
LEVELS = 8
FINE = 3
SUBLANES = 8
MIX_PIECE = 256
PROJ_PIECE = 256
VMEM_LIMIT = 56 * 1024 * 1024

TILES = dict(
    mod_tn=1024,
    proj_tm=2048,
    norm_tm=1024,
    conv_tm=1024,
    conv_cw=512,
    gate_tm=1024,
    gate_tn=1024,
    mix_tm=512,
    ffn_tm=1024,
    ffn_tf=512,
)


def _cparams(*sem):
    return pltpu.CompilerParams(dimension_semantics=sem, vmem_limit_bytes=VMEM_LIMIT)


def _dot(a, b):
    return jnp.dot(a, b, preferred_element_type=F32)


def _dot_nt(a, b):
    return lax.dot_general(a, b, (((1,), (1,)), ((), ())), preferred_element_type=F32)


def _silu(x):
    return x * jax.nn.sigmoid(x)


def _rms(x):
    return x * lax.rsqrt(jnp.mean(x * x, axis=-1, keepdims=True) + EPS)


def _mod_kernel(c_ref, w_ref, b_ref, o_ref):
    a = _silu(c_ref[...])
    nb = a.shape[0]
    a_hi = a.astype(BF16)
    a2 = jnp.concatenate([a_hi, (a - a_hi.astype(F32)).astype(BF16)], axis=0)
    w = w_ref[...]
    w_hi = w.astype(BF16)
    w_lo = (w - w_hi.astype(F32)).astype(BF16)
    acc = _dot(a2, w_hi) + _dot(a2, w_lo)
    o_ref[...] = acc[:nb] + acc[nb:] + b_ref[...]


def _modulation(c, w_ada, b_ada):
    b, d = c.shape
    n = w_ada.shape[1]
    tn = TILES["mod_tn"]
    return pl.pallas_call(
        _mod_kernel,
        grid=(n // tn,),
        in_specs=[pl.BlockSpec((b, d), lambda j: (0, 0)),
                  pl.BlockSpec((d, tn), lambda j: (0, j)),
                  pl.BlockSpec((1, tn), lambda j: (0, j))],
        out_specs=pl.BlockSpec((b, tn), lambda j: (0, j)),
        out_shape=jax.ShapeDtypeStruct((b, n), F32),
        compiler_params=_cparams("parallel"),
        name="mod",
    )(c, w_ada, b_ada.reshape(1, n))


def _rows_lhs(h_ref, rows, extra, outs):
    return h_ref[rows, :]


def _norm_lhs(x_ref, rows, extra, outs):
    g_ref, sc_ref, sh_ref = extra
    y = _rms(x_ref[rows, :]) * g_ref[...]
    h = (y * (1.0 + sc_ref[...]) + sh_ref[...]).astype(BF16)
    outs[-1][rows, :] = h
    return h


def _proj_kernel(*refs, n_w, n_extra, n_side, n_out, epilogue, lhs):
    src_ref = refs[0]
    pos = 1
    w_refs = refs[pos:pos + n_w]
    pos += n_w
    extra = refs[pos:pos + n_extra]
    pos += n_extra
    side_in = refs[pos:pos + n_side]
    pos += n_side
    outs = refs[pos:pos + n_out]
    pos += n_out
    side_out = refs[pos:pos + n_side]
    pos += n_side
    wbf = refs[pos]
    scratch = refs[pos + 1:]

    @pl.when(pl.program_id(1) == 0)
    def _():
        off = 0
        for w_ref in w_refs:
            wd = w_ref.shape[1]
            wbf[:, off:off + wd] = w_ref[...].astype(BF16)
            off += wd

    for s_ref, o_ref in zip(side_in, side_out):
        o_ref[...] = s_ref[...].astype(BF16)

    tm = src_ref.shape[0]
    for r0 in range(0, tm, PROJ_PIECE):
        rows = slice(r0, r0 + PROJ_PIECE)
        epilogue(_dot(lhs(src_ref, rows, extra, outs), wbf[...]), rows, tm, extra, outs, scratch)


def _proj_call(epilogue, h, w, col_starts, wd, nj, extras, extra_specs, out_widths, out_dtypes,
               tm, name, scratch_shapes=(), lhs=_rows_lhs, side_casts=()):
    t, d = h.shape
    ni = t // tm
    in_specs = [pl.BlockSpec((tm, d), lambda j, i: (i, 0))]
    for c0 in col_starts:
        in_specs.append(pl.BlockSpec((d, wd), functools.partial(lambda j, i, b: (0, b + j), b=c0 // wd)))
    in_specs += list(extra_specs)
    out_specs = [pl.BlockSpec((tm, ow), lambda j, i: (i, j)) for ow in out_widths]
    out_shape = [jax.ShapeDtypeStruct((t, nj * ow), dt) for ow, dt in zip(out_widths, out_dtypes)]
    for a in side_casts:
        rb = a.shape[0] // (nj * ni)
        assert rb * nj * ni == a.shape[0] and rb % (2 * SUBLANES) == 0
        spec = pl.BlockSpec((rb, a.shape[1]), lambda j, i: (j * ni + i, 0))
        in_specs.append(spec)
        out_specs.append(spec)
        out_shape.append(jax.ShapeDtypeStruct(a.shape, BF16))
    kern = functools.partial(_proj_kernel, n_w=len(col_starts), n_extra=len(extras),
                             n_side=len(side_casts), n_out=len(out_widths), epilogue=epilogue,
                             lhs=lhs)
    return pl.pallas_call(
        kern,
        grid=(nj, ni),
        in_specs=in_specs,
        out_specs=out_specs,
        out_shape=out_shape,
        scratch_shapes=[pltpu.VMEM((d, wd * len(col_starts)), BF16)] + list(scratch_shapes),
        compiler_params=_cparams("arbitrary", "arbitrary"),
        name=name,
    )(h, *([w] * len(col_starts)), *extras, *side_casts)


def _conv_epilogue(acc, rows, tm, extra, outs, scratch, *, cw, tiles_per_seq):
    cw_ref, = extra
    o_ref, = outs
    ubuf, = scratch
    r0, n = rows.start, rows.stop - rows.start
    u = acc[:, cw:2 * cw] * acc[:, 2 * cw:]

    if r0 == 0:
        @pl.when(pl.program_id(1) % tiles_per_seq == 0)
        def _():
            ubuf[0:SUBLANES, :] = jnp.zeros((SUBLANES, cw), F32)

    ubuf[SUBLANES + r0:SUBLANES + r0 + n, :] = u
    wk = cw_ref[...]
    y = wk[2:3, :] * u
    y = y + wk[1:2, :] * ubuf[SUBLANES - 1 + r0:SUBLANES - 1 + r0 + n, :]
    y = y + wk[0:1, :] * ubuf[SUBLANES - 2 + r0:SUBLANES - 2 + r0 + n, :]
    o_ref[rows, :] = (acc[:, :cw] * y).astype(BF16)
    if rows.stop == tm:
        ubuf[0:SUBLANES, :] = ubuf[tm:tm + SUBLANES, :]


def _silu_epilogue(acc, rows, tm, extra, outs, scratch):
    outs[0][rows, :] = _silu(acc).astype(BF16)


def _gate_gain_epilogue(acc, rows, tm, extra, outs, scratch):
    outs[0][rows, :] = (_silu(acc) * extra[0][...]).astype(BF16)


def _ident_epilogue(acc, rows, tm, extra, outs, scratch):
    outs[0][rows, :] = acc.astype(BF16)


def _sigmoid_epilogue(acc, rows, tm, extra, outs, scratch):
    outs[0][rows, :] = jax.nn.sigmoid(acc).astype(BF16)


def _forget_epilogue(acc, rows, tm, extra, outs, scratch, *, layer):
    k_ref, lf_ref = outs
    p = extra[0][...]
    e = jnp.exp(p - jnp.max(p, axis=0, keepdims=True))
    sm = e / jnp.sum(e, axis=0, keepdims=True)
    lb = jnp.sum(sm[:layer + 1, :], axis=0, keepdims=True)
    f = lb + (1.0 - lb) * jax.nn.sigmoid(acc)
    k_ref[rows, :] = (1.0 - f).astype(BF16)
    lf_ref[rows, :] = jnp.log2(f)


def _level_tables():
    n = CHUNK
    level = -np.ones((n, n), np.int32)
    expo = np.zeros((FINE + 1, n, n), np.float32)
    for t in range(n):
        level[t, t] = 0
        for s in range(t):
            level[t, s] = (t ^ s).bit_length()
        for li in range(1, FINE + 1):
            m = 1 << (li - 1)
            blk = (t // m) * m
            if (t % (2 * m)) >= m:
                expo[li - 1, t, blk:t + 1] = 1.0
            else:
                expo[li - 1, t, t + 1:blk + m] = 1.0
        expo[FINE, t, :t + 1] = 1.0
    expo = expo.reshape((FINE + 1) * n, n)
    return level, np.concatenate([expo, expo], axis=1)


def _hgrn_kernel(q_ref, k_ref, lf_ref, v_ref, gs_ref, lv_ref, ex_ref, o_ref, st_ref, dec_ref):
    c = pl.program_id(1)

    @pl.when(c == 0)
    def _():
        st_ref[...] = jnp.zeros_like(st_ref)

    n = CHUNK
    nb = n // SUBLANES
    g0 = FINE * n
    tile = lambda a, r: a[r * SUBLANES:(r + 1) * SUBLANES, :]

    lf = lf_ref[...]
    hi = lf.astype(BF16)
    lo = (lf - hi.astype(F32)).astype(BF16)
    dec_ref[...] = _dot(ex_ref[...], jnp.concatenate([hi, lo], axis=0))

    def pair_dot_nt(lhs2, rhs2):
        a0, a1 = lhs2
        z = jnp.zeros_like(a0)
        lhs = jnp.concatenate([jnp.concatenate([a0, z], axis=1),
                               jnp.concatenate([z, a1], axis=1)], axis=0)
        p = _dot_nt(lhs, jnp.concatenate(rhs2, axis=1))
        return p[:a0.shape[0]], p[a0.shape[0]:]

    for hd0 in range(0, HEADS, 2):
        pair = (hd0, hd0 + 1)
        sls = [slice(hd * HEAD_DIM, (hd + 1) * HEAD_DIM) for hd in pair]
        qb = [q_ref[:, sl] for sl in sls]
        kb = [k_ref[:, sl] for sl in sls]
        q = [a.astype(F32) for a in qb]
        k = [a.astype(F32) for a in kb]
        g = [dec_ref[g0:g0 + n, sl] for sl in sls]

        def halves_exponent(m, h):
            parts = []
            for r0 in range(0, n, 2 * m):
                g_n = dec_ref[g0 + r0 + m - 1:g0 + r0 + m, sls[h]]
                parts += [g_n - g[h][r0:r0 + m], g[h][r0 + m:r0 + 2 * m] - g_n]
            return jnp.concatenate(parts, axis=0)

        p2 = pair_dot_nt(qb, kb)
        sc = [[jnp.where(tile(lv_ref, r) == 0, tile(p, r), 0.0) for r in range(nb)] for p in p2]
        for li in range(1, LEVELS):
            m = 1 << (li - 1)
            lhs2, rhs2 = [], []
            for h in range(2):
                x = dec_ref[(li - 1) * n:li * n, sls[h]] if li <= FINE else halves_exponent(m, h)
                e = jnp.exp2(x)
                if m < SUBLANES:
                    rows = list(range(nb))
                    lhs2.append((q[h] * e).astype(BF16))
                    rhs2.append((k[h] * e).astype(BF16))
                else:
                    q_parts, k_parts, rows = [], [], []
                    for r0 in range(0, n, 2 * m):
                        k_parts += [k[h][r0:r0 + m] * e[r0:r0 + m], jnp.zeros((m, HEAD_DIM), F32)]
                        q_parts += [q[h][r0 + m:r0 + 2 * m] * e[r0 + m:r0 + 2 * m]]
                        rows += list(range((r0 + m) // SUBLANES, (r0 + 2 * m) // SUBLANES))
                    lhs2.append(jnp.concatenate(q_parts, axis=0).astype(BF16))
                    rhs2.append(jnp.concatenate(k_parts, axis=0).astype(BF16))
            p2 = pair_dot_nt(lhs2, rhs2)
            for h in range(2):
                for i, r in enumerate(rows):
                    sc[h][r] = jnp.where(tile(lv_ref, r) == li, tile(p2[h], i), sc[h][r])

        g_last = [dec_ref[g0 + n - 1:g0 + n, sl] for sl in sls]
        st = [st_ref[hd] for hd in pair]
        o_in = pair_dot_nt([(q[h] * jnp.exp2(g[h])).astype(BF16) for h in range(2)],
                           [a.astype(BF16) for a in st])
        for h in range(2):
            vb = v_ref[:, sls[h]]
            o = o_in[h] + _dot(jnp.concatenate(sc[h], axis=0).astype(BF16), vb)
            k_out = (k[h] * jnp.exp2(g_last[h] - g[h])).astype(BF16)
            vt = vb.astype(F32).T.astype(BF16)
            st_ref[pair[h]] = st[h] * jnp.exp2(g_last[h]) + _dot(vt, k_out)
            o_ref[:, sls[h]] = (_rms(o) * gs_ref[:, sls[h]].astype(F32)).astype(BF16)


def _hgrn(q, k, lf, v, gs, batch, seq):
    t, dk = q.shape
    nc = seq // CHUNK
    level, expo = _level_tables()
    row = lambda b, c: (b * nc + c, 0)
    const = lambda b, c: (0, 0)
    blk = pl.BlockSpec((CHUNK, dk), row)
    return pl.pallas_call(
        _hgrn_kernel,
        grid=(batch, nc),
        in_specs=[blk, blk, blk, blk, blk,
                  pl.BlockSpec((CHUNK, CHUNK), const),
                  pl.BlockSpec(expo.shape, const)],
        out_specs=blk,
        out_shape=jax.ShapeDtypeStruct((t, dk), BF16),
        scratch_shapes=[pltpu.VMEM((HEADS, HEAD_DIM, HEAD_DIM), F32),
                        pltpu.VMEM(((FINE + 1) * CHUNK, dk), F32)],
        compiler_params=_cparams("arbitrary", "arbitrary"),
        name="hgrn",
    )(q, k, lf, v, gs, jnp.asarray(level), jnp.asarray(expo, BF16))


def _mix_kernel(z_ref, on_ref, sg_ref, x_ref, wc_ref, wh_ref, wo_ref, gt_ref, g2_ref, sc_ref,
                sh_ref, x1_ref, h2_ref, *, d):
    for r0 in range(0, z_ref.shape[0], MIX_PIECE):
        rows = slice(r0, r0 + MIX_PIECE)
        y_a = _dot(z_ref[rows, :], wc_ref[...])
        y_b = _dot(on_ref[rows, :], wh_ref[...])
        merged = sg_ref[rows, :d].astype(F32) * y_a + sg_ref[rows, d:].astype(F32) * y_b
        x1 = x_ref[rows, :] + gt_ref[...] * _dot(merged.astype(BF16), wo_ref[...])
        x1_ref[rows, :] = x1
        h2_ref[rows, :] = (_rms(x1) * g2_ref[...] * (1.0 + sc_ref[...]) + sh_ref[...]).astype(BF16)


def _mix(z, on, sg, x, wc, wh, wo, gt, g2, sc, sh, seq):
    t, d = x.shape
    tm = TILES["mix_tm"]
    assert tm % MIX_PIECE == 0
    per = seq // tm
    row = lambda i: (i, 0)
    const = lambda i: (0, 0)
    vec = pl.BlockSpec((None, 1, d), lambda i: (i // per, 0, 0))
    resident = lambda shape: pl.BlockSpec(shape, const, pipeline_mode=pl.Buffered(1))
    return pl.pallas_call(
        functools.partial(_mix_kernel, d=d),
        grid=(t // tm,),
        in_specs=[pl.BlockSpec((tm, z.shape[1]), row),
                  pl.BlockSpec((tm, on.shape[1]), row),
                  pl.BlockSpec((tm, 2 * d), row),
                  pl.BlockSpec((tm, d), row),
                  resident(wc.shape), resident(wh.shape), resident(wo.shape),
                  vec, pl.BlockSpec((1, d), const), vec, vec],
        out_specs=[pl.BlockSpec((tm, d), row), pl.BlockSpec((tm, d), row)],
        out_shape=[jax.ShapeDtypeStruct((t, d), F32), jax.ShapeDtypeStruct((t, d), BF16)],
        compiler_params=_cparams("parallel"),
        name="mix",
    )(z, on, sg, x, wc, wh, wo, gt, g2.reshape(1, d), sc, sh)


def _ffn_kernel(h_ref, wg_ref, wu_ref, wd_ref, x1_hbm, gt_ref, gf_ref, o_ref, x1_buf, x1_sem, *, tm):
    i = pl.program_id(0)
    j = pl.program_id(1)

    def x1_copy():
        return pltpu.make_async_copy(x1_hbm.at[pl.ds(i * tm, tm), :], x1_buf, x1_sem)

    halves = [slice(r0, r0 + tm // 2) for r0 in range(0, tm, tm // 2)]

    def down(rows):
        h = h_ref[rows, :]
        act = (_silu(_dot(h, wg_ref[...])) * _dot(h, wu_ref[...])).astype(BF16)
        return _dot(act, wd_ref[...])

    last = pl.num_programs(1) - 1

    @pl.when(j == 0)
    def _():
        x1_copy().start()
        for rows in halves:
            o_ref[rows, :] = down(rows)

    @pl.when(jnp.logical_and(j > 0, j < last))
    def _():
        for rows in halves:
            o_ref[rows, :] += down(rows)

    @pl.when(j == last)
    def _():
        x1_copy().wait()
        for rows in halves:
            x2 = x1_buf[rows, :] + gt_ref[...] * (o_ref[rows, :] + down(rows))
            o_ref[rows, :] = _rms(x2) * gf_ref[...]


def _ffn(h2, wg, wu, wd, x1, gt, gf, seq):
    t, d = x1.shape
    dff = wg.shape[1]
    tm, tf = TILES["ffn_tm"], TILES["ffn_tf"]
    per = seq // tm
    return pl.pallas_call(
        functools.partial(_ffn_kernel, tm=tm),
        grid=(t // tm, dff // tf),
        in_specs=[pl.BlockSpec((tm, d), lambda i, j: (i, 0)),
                  pl.BlockSpec((d, tf), lambda i, j: (0, j)),
                  pl.BlockSpec((d, tf), lambda i, j: (0, j)),
                  pl.BlockSpec((tf, d), lambda i, j: (j, 0)),
                  pl.BlockSpec(memory_space=pl.ANY),
                  pl.BlockSpec((None, 1, d), lambda i, j: (i // per, 0, 0)),
                  pl.BlockSpec((1, d), lambda i, j: (0, 0))],
        out_specs=pl.BlockSpec((tm, d), lambda i, j: (i, 0)),
        out_shape=jax.ShapeDtypeStruct((t, d), F32),
        scratch_shapes=[pltpu.VMEM((tm, d), F32), pltpu.SemaphoreType.DMA(())],
        compiler_params=_cparams("arbitrary", "arbitrary"),
        name="ffn",
    )(h2, wg, wu, wd, x1, gt, gf.reshape(1, d))


def kernel(x, c, w_ada, b_ada, norm_mix_g, w_in, conv_w, lb_param, gnorm_g, w_conv_out,
           w_hgrn_out, w_o, norm_ffn_g, w_ffn_gate, w_ffn_up, w_ffn_down, norm_final_g):
    b, s, d = x.shape
    t = b * s
    depth = w_in.shape[0]
    dc = conv_w.shape[2]
    dk = lb_param.shape[1]
    dv = w_hgrn_out.shape[1]
    assert dk == HEADS * HEAD_DIM and dv == HEADS * HEAD_DIM and s % CHUNK == 0
    tm, tm_conv, tm_norm, tm_gate = (TILES[key] for key in ("proj_tm", "conv_tm", "norm_tm", "gate_tm"))
    cw = TILES["conv_cw"]
    assert dc % cw == 0 and all(s % r == 0 and r % PROJ_PIECE == 0
                                for r in (tm, tm_conv, tm_norm, tm_gate))

    assert depth == 1
    l = 0

    mod = _modulation(c, w_ada[l], b_ada[l]).reshape(b, 6, 1, d)
    sh_m, sc_m, gt_m, sh_f, sc_f, gt_f = [mod[:, i] for i in range(6)]

    w = w_in[l]
    c_q = 3 * dc
    c_f, c_i, c_g, c_gate = c_q + dk, c_q + 2 * dk, c_q + 2 * dk + dv, c_q + 2 * dk + 2 * dv
    tg = TILES["gate_tn"]
    assert (w.shape[1] - c_gate) % tg == 0 and c_gate % tg == 0 and c_q % dk == 0 and dk == dv

    x2d = x.reshape(t, d)
    vec = pl.BlockSpec((None, 1, d), lambda j, i: (i // (s // tm_norm), 0, 0))
    q, h = _proj_call(
        _silu_epilogue, x2d, w, [c_q], dk, 1, [norm_mix_g[l].reshape(1, d), sc_m, sh_m],
        [pl.BlockSpec((1, d), lambda j, i: (0, 0)), vec, vec], [dk, d], [BF16, BF16], tm_norm,
        "proj_q", lhs=_norm_lhs)
    z, wd_b = _proj_call(
        functools.partial(_conv_epilogue, cw=cw, tiles_per_seq=s // tm_conv),
        h, w, [0, dc, 2 * dc], cw, dc // cw, [conv_w[l]],
        [pl.BlockSpec((CONV_WIDTH, cw), lambda j, i: (0, j))], [cw], [BF16], tm_conv, "proj_conv",
        scratch_shapes=[pltpu.VMEM((tm_conv + SUBLANES, cw), F32)], side_casts=[w_ffn_down[l]])
    kk, lf = _proj_call(
        functools.partial(_forget_epilogue, layer=l), h, w, [c_f], dk, 1, [lb_param],
        [pl.BlockSpec((lb_param.shape[0], dk), lambda j, i: (0, 0))],
        [dk, dk], [BF16, F32], tm, "proj_f")
    v, wc_b, wh_b = _proj_call(_ident_epilogue, h, w, [c_i], dv, 1, [], [], [dv], [BF16], tm,
                               "proj_i", side_casts=[w_conv_out[l], w_hgrn_out[l]])
    gs, wo_b = _proj_call(_gate_gain_epilogue, h, w, [c_g], dv, 1,
                          [jnp.tile(gnorm_g[l], HEADS).reshape(1, dv)],
                          [pl.BlockSpec((1, dv), lambda j, i: (0, 0))], [dv], [BF16], tm, "proj_g",
                          side_casts=[w_o[l]])
    sg, wg_b, wu_b = _proj_call(_sigmoid_epilogue, h, w, [c_gate], tg, (w.shape[1] - c_gate) // tg,
                                [], [], [tg], [BF16], tm_gate, "proj_gate",
                                side_casts=[w_ffn_gate[l], w_ffn_up[l]])

    on = _hgrn(q, kk, lf, v, gs, b, s)

    x1, h2 = _mix(z, on, sg, x2d, wc_b, wh_b, wo_b, gt_m, norm_ffn_g[l], sc_f, sh_f, s)
    out = _ffn(h2, wg_b, wu_b, wd_b, x1, gt_f, norm_final_g, s)
    return out.reshape(b, s, d)
```

```python
import functools

import jax
import jax.numpy as jnp
import numpy as np
from jax import lax
from jax.experimental import pallas as pl
from jax.experimental.pallas import tpu as pltpu

F32 = jnp.float32
BF16 = jnp.bfloat16

EPS = 1e-6
CONV_WIDTH = 3
HEADS = 8
HEAD_DIM = 128
CHUNK = 128
LEVELS = 8
FINE = 3
SUBLANES = 8
MIX_PIECE = 256
PROJ_PIECE = 256
VMEM_LIMIT = 56 * 1024 * 1024

TILES = dict(
    mod_tn=1024,
    proj_tm=2048,
    norm_tm=1024,
    conv_tm=1024,
    conv_cw=512,
    gate_tm=2048,
    gate_tn=1024,
    hgrn_rows=512,
    mix_tm=512,
    ffn_tm=1024,
    ffn_tf=512,
)


def _cparams(*sem):
    return pltpu.CompilerParams(dimension_semantics=sem, vmem_limit_bytes=VMEM_LIMIT)


def _dot(a, b):
    return jnp.dot(a, b, preferred_element_type=F32)


def _dot_nt(a, b):
    return lax.dot_general(a, b, (((1,), (1,)), ((), ())), preferred_element_type=F32)


def _silu(x):
    return x * jax.nn.sigmoid(x)


def _rms(x):
    return x * lax.rsqrt(jnp.mean(x * x, axis=-1, keepdims=True) + EPS)


def _mod_kernel(c_ref, w_ref, b_ref, o_ref):
    a = _silu(c_ref[...])
    nb = a.shape[0]
    a_hi = a.astype(BF16)
    a2 = jnp.concatenate([a_hi, (a - a_hi.astype(F32)).astype(BF16)], axis=0)
    w = w_ref[...]
    w_hi = w.astype(BF16)
    w_lo = (w - w_hi.astype(F32)).astype(BF16)
    acc = _dot(a2, w_hi) + _dot(a2, w_lo)
    o_ref[...] = acc[:nb] + acc[nb:] + b_ref[...]


def _modulation(c, w_ada, b_ada):
    b, d = c.shape
    n = w_ada.shape[1]
    tn = TILES["mod_tn"]
    return pl.pallas_call(
        _mod_kernel,
        grid=(n // tn,),
        in_specs=[pl.BlockSpec((b, d), lambda j: (0, 0)),
                  pl.BlockSpec((d, tn), lambda j: (0, j)),
                  pl.BlockSpec((1, tn), lambda j: (0, j))],
        out_specs=pl.BlockSpec((b, tn), lambda j: (0, j)),
        out_shape=jax.ShapeDtypeStruct((b, n), F32),
        compiler_params=_cparams("parallel"),
        name="mod",
    )(c, w_ada, b_ada.reshape(1, n))


def _rows_lhs(h_ref, rows, extra, outs):
    return h_ref[rows, :]


def _norm_lhs(x_ref, rows, extra, outs):
    g_ref, sc_ref, sh_ref = extra
    y = _rms(x_ref[rows, :]) * g_ref[...]
    h = (y * (1.0 + sc_ref[...]) + sh_ref[...]).astype(BF16)
    outs[-1][rows, :] = h
    return h


def _proj_kernel(*refs, n_w, n_extra, n_side, n_out, epilogue, lhs):
    src_ref = refs[0]
    pos = 1
    w_refs = refs[pos:pos + n_w]
    pos += n_w
    extra = refs[pos:pos + n_extra]
    pos += n_extra
    side_in = refs[pos:pos + n_side]
    pos += n_side
    outs = refs[pos:pos + n_out]
    pos += n_out
    side_out = refs[pos:pos + n_side]
    pos += n_side
    wbf = refs[pos]
    scratch = refs[pos + 1:]

    @pl.when(pl.program_id(1) == 0)
    def _():
        off = 0
        for w_ref in w_refs:
            wd = w_ref.shape[1]
            wbf[:, off:off + wd] = w_ref[...].astype(BF16)
            off += wd

    for s_ref, o_ref in zip(side_in, side_out):
        o_ref[...] = s_ref[...].astype(BF16)

    tm = src_ref.shape[0]
    for r0 in range(0, tm, PROJ_PIECE):
        rows = slice(r0, r0 + PROJ_PIECE)
        epilogue(_dot(lhs(src_ref, rows, extra, outs), wbf[...]), rows, tm, extra, outs, scratch)


def _proj_call(epilogue, h, w, col_starts, wd, nj, extras, extra_specs, out_widths, out_dtypes,
               tm, name, scratch_shapes=(), lhs=_rows_lhs, side_casts=()):
    t, d = h.shape
    ni = t // tm
    in_specs = [pl.BlockSpec((tm, d), lambda j, i: (i, 0))]
    for c0 in col_starts:
        in_specs.append(pl.BlockSpec((d, wd), functools.partial(lambda j, i, b: (0, b + j), b=c0 // wd)))
    in_specs += list(extra_specs)
    out_specs = [pl.BlockSpec((tm, ow), lambda j, i: (i, j)) for ow in out_widths]
    out_shape = [jax.ShapeDtypeStruct((t, nj * ow), dt) for ow, dt in zip(out_widths, out_dtypes)]
    for a in side_casts:
        rb = a.shape[0] // (nj * ni)
        assert rb * nj * ni == a.shape[0] and rb % (2 * SUBLANES) == 0
        spec = pl.BlockSpec((rb, a.shape[1]), lambda j, i: (j * ni + i, 0))
        in_specs.append(spec)
        out_specs.append(spec)
        out_shape.append(jax.ShapeDtypeStruct(a.shape, BF16))
    kern = functools.partial(_proj_kernel, n_w=len(col_starts), n_extra=len(extras),
                             n_side=len(side_casts), n_out=len(out_widths), epilogue=epilogue,
                             lhs=lhs)
    return pl.pallas_call(
        kern,
        grid=(nj, ni),
        in_specs=in_specs,
        out_specs=out_specs,
        out_shape=out_shape,
        scratch_shapes=[pltpu.VMEM((d, wd * len(col_starts)), BF16)] + list(scratch_shapes),
        compiler_params=_cparams("arbitrary", "arbitrary"),
        name=name,
    )(h, *([w] * len(col_starts)), *extras, *side_casts)


def _conv_epilogue(acc, rows, tm, extra, outs, scratch, *, cw, tiles_per_seq):
    cw_ref, = extra
    o_ref, = outs
    ubuf, = scratch
    r0, n = rows.start, rows.stop - rows.start
    u = acc[:, cw:2 * cw] * acc[:, 2 * cw:]

    if r0 == 0:
        @pl.when(pl.program_id(1) % tiles_per_seq == 0)
        def _():
            ubuf[0:SUBLANES, :] = jnp.zeros((SUBLANES, cw), F32)

    ubuf[SUBLANES + r0:SUBLANES + r0 + n, :] = u
    wk = cw_ref[...]
    y = wk[2:3, :] * u
    y = y + wk[1:2, :] * ubuf[SUBLANES - 1 + r0:SUBLANES - 1 + r0 + n, :]
    y = y + wk[0:1, :] * ubuf[SUBLANES - 2 + r0:SUBLANES - 2 + r0 + n, :]
    o_ref[rows, :] = (acc[:, :cw] * y).astype(BF16)
    if rows.stop == tm:
        ubuf[0:SUBLANES, :] = ubuf[tm:tm + SUBLANES, :]


def _silu_epilogue(acc, rows, tm, extra, outs, scratch):
    outs[0][rows, :] = _silu(acc).astype(BF16)


def _gate_gain_epilogue(acc, rows, tm, extra, outs, scratch):
    outs[0][rows, :] = (_silu(acc) * extra[0][...]).astype(BF16)


def _ident_epilogue(acc, rows, tm, extra, outs, scratch):
    outs[0][rows, :] = acc.astype(BF16)


def _sigmoid_epilogue(acc, rows, tm, extra, outs, scratch):
    outs[0][rows, :] = jax.nn.sigmoid(acc).astype(BF16)


def _forget_epilogue(acc, rows, tm, extra, outs, scratch, *, layer):
    k_ref, lf_ref = outs
    p = extra[0][...]
    e = jnp.exp(p - jnp.max(p, axis=0, keepdims=True))
    sm = e / jnp.sum(e, axis=0, keepdims=True)
    lb = jnp.sum(sm[:layer + 1, :], axis=0, keepdims=True)
    f = lb + (1.0 - lb) * jax.nn.sigmoid(acc)
    k_ref[rows, :] = (1.0 - f).astype(BF16)
    lf_ref[rows, :] = jnp.log2(f)


def _level_tables():
    n = CHUNK
    level = -np.ones((n, n), np.int32)
    expo = np.zeros((FINE + 1, n, n), np.float32)
    for t in range(n):
        level[t, t] = 0
        for s in range(t):
            level[t, s] = (t ^ s).bit_length()
        for li in range(1, FINE + 1):
            m = 1 << (li - 1)
            blk = (t // m) * m
            if (t % (2 * m)) >= m:
                expo[li - 1, t, blk:t + 1] = 1.0
            else:
                expo[li - 1, t, t + 1:blk + m] = 1.0
        expo[FINE, t, :t + 1] = 1.0
    expo = expo.reshape((FINE + 1) * n, n)
    return level, np.concatenate([expo, expo], axis=1)


def _hgrn_kernel(q_ref, k_ref, lf_ref, v_ref, gs_ref, lv_ref, ex_ref, o_ref, st_ref, dec_ref):
    @pl.when(pl.program_id(1) == 0)
    def _():
        st_ref[...] = jnp.zeros_like(st_ref)

    def chunk(ci, carry):
        rows = pl.ds(pl.multiple_of(ci * CHUNK, CHUNK), CHUNK)
        _hgrn_chunk(rows, q_ref, k_ref, lf_ref, v_ref, gs_ref, lv_ref, ex_ref, o_ref, st_ref, dec_ref)
        return carry

    lax.fori_loop(0, q_ref.shape[0] // CHUNK, chunk, 0)


def _hgrn_chunk(crows, q_ref, k_ref, lf_ref, v_ref, gs_ref, lv_ref, ex_ref, o_ref, st_ref, dec_ref):
    n = CHUNK
    nb = n // SUBLANES
    g0 = FINE * n
    tile = lambda a, r: a[r * SUBLANES:(r + 1) * SUBLANES, :]

    lf = lf_ref[crows, :]
    hi = lf.astype(BF16)
    lo = (lf - hi.astype(F32)).astype(BF16)
    dec_ref[...] = _dot(ex_ref[...], jnp.concatenate([hi, lo], axis=0))

    def pair_dot_nt(lhs2, rhs2):
        a0, a1 = lhs2
        z = jnp.zeros_like(a0)
        lhs = jnp.concatenate([jnp.concatenate([a0, z], axis=1),
                               jnp.concatenate([z, a1], axis=1)], axis=0)
        p = _dot_nt(lhs, jnp.concatenate(rhs2, axis=1))
        return p[:a0.shape[0]], p[a0.shape[0]:]

    for hd0 in range(0, HEADS, 2):
        pair = (hd0, hd0 + 1)
        sls = [slice(hd * HEAD_DIM, (hd + 1) * HEAD_DIM) for hd in pair]
        qb = [q_ref[crows, sl] for sl in sls]
        kb = [k_ref[crows, sl] for sl in sls]
        q = [a.astype(F32) for a in qb]
        k = [a.astype(F32) for a in kb]
        g = [dec_ref[g0:g0 + n, sl] for sl in sls]

        def halves_exponent(m, h):
            parts = []
            for r0 in range(0, n, 2 * m):
                g_n = dec_ref[g0 + r0 + m - 1:g0 + r0 + m, sls[h]]
                parts += [g_n - g[h][r0:r0 + m], g[h][r0 + m:r0 + 2 * m] - g_n]
            return jnp.concatenate(parts, axis=0)

        p2 = pair_dot_nt(qb, kb)
        sc = [[jnp.where(tile(lv_ref, r) == 0, tile(p, r), 0.0) for r in range(nb)] for p in p2]
        for li in range(1, LEVELS):
            m = 1 << (li - 1)
            lhs2, rhs2 = [], []
            for h in range(2):
                x = dec_ref[(li - 1) * n:li * n, sls[h]] if li <= FINE else halves_exponent(m, h)
                e = jnp.exp2(x)
                if m < SUBLANES:
                    rows = list(range(nb))
                    lhs2.append((q[h] * e).astype(BF16))
                    rhs2.append((k[h] * e).astype(BF16))
                else:
                    q_parts, k_parts, rows = [], [], []
                    for r0 in range(0, n, 2 * m):
                        k_parts += [k[h][r0:r0 + m] * e[r0:r0 + m], jnp.zeros((m, HEAD_DIM), F32)]
                        q_parts += [q[h][r0 + m:r0 + 2 * m] * e[r0 + m:r0 + 2 * m]]
                        rows += list(range((r0 + m) // SUBLANES, (r0 + 2 * m) // SUBLANES))
                    lhs2.append(jnp.concatenate(q_parts, axis=0).astype(BF16))
                    rhs2.append(jnp.concatenate(k_parts, axis=0).astype(BF16))
            p2 = pair_dot_nt(lhs2, rhs2)
            for h in range(2):
                for i, r in enumerate(rows):
                    sc[h][r] = jnp.where(tile(lv_ref, r) == li, tile(p2[h], i), sc[h][r])

        g_last = [dec_ref[g0 + n - 1:g0 + n, sl] for sl in sls]
        st = [st_ref[hd] for hd in pair]
        o_in = pair_dot_nt([(q[h] * jnp.exp2(g[h])).astype(BF16) for h in range(2)],
                           [a.astype(BF16) for a in st])
        for h in range(2):
            vb = v_ref[crows, sls[h]]
            o = o_in[h] + _dot(jnp.concatenate(sc[h], axis=0).astype(BF16), vb)
            k_out = (k[h] * jnp.exp2(g_last[h] - g[h])).astype(BF16)
            vt = vb.astype(F32).T.astype(BF16)
            st_ref[pair[h]] = st[h] * jnp.exp2(g_last[h]) + _dot(vt, k_out)
            o_ref[crows, sls[h]] = (_rms(o) * gs_ref[crows, sls[h]].astype(F32)).astype(BF16)


def _hgrn(q, k, lf, v, gs, batch, seq):
    t, dk = q.shape
    tr = TILES["hgrn_rows"]
    assert seq % tr == 0 and tr % CHUNK == 0
    nc = seq // tr
    level, expo = _level_tables()
    row = lambda b, c: (b * nc + c, 0)
    const = lambda b, c: (0, 0)
    blk = pl.BlockSpec((tr, dk), row)
    return pl.pallas_call(
        _hgrn_kernel,
        grid=(batch, nc),
        in_specs=[blk, blk, blk, blk, blk,
                  pl.BlockSpec((CHUNK, CHUNK), const),
                  pl.BlockSpec(expo.shape, const)],
        out_specs=blk,
        out_shape=jax.ShapeDtypeStruct((t, dk), BF16),
        scratch_shapes=[pltpu.VMEM((HEADS, HEAD_DIM, HEAD_DIM), F32),
                        pltpu.VMEM(((FINE + 1) * CHUNK, dk), F32)],
        compiler_params=_cparams("arbitrary", "arbitrary"),
        name="hgrn",
    )(q, k, lf, v, gs, jnp.asarray(level), jnp.asarray(expo, BF16))


def _mix_kernel(z_ref, on_ref, sg_ref, x_ref, wc_ref, wh_ref, wo_ref, gt_ref, g2_ref, sc_ref,
                sh_ref, x1_ref, h2_ref, *, d):
    for r0 in range(0, z_ref.shape[0], MIX_PIECE):
        rows = slice(r0, r0 + MIX_PIECE)
        y_a = _dot(z_ref[rows, :], wc_ref[...])
        y_b = _dot(on_ref[rows, :], wh_ref[...])
        merged = sg_ref[rows, :d].astype(F32) * y_a + sg_ref[rows, d:].astype(F32) * y_b
        x1 = x_ref[rows, :] + gt_ref[...] * _dot(merged.astype(BF16), wo_ref[...])
        x1_ref[rows, :] = x1
        h2_ref[rows, :] = (_rms(x1) * g2_ref[...] * (1.0 + sc_ref[...]) + sh_ref[...]).astype(BF16)


def _mix(z, on, sg, x, wc, wh, wo, gt, g2, sc, sh, seq):
    t, d = x.shape
    tm = TILES["mix_tm"]
    assert tm % MIX_PIECE == 0
    per = seq // tm
    row = lambda i: (i, 0)
    const = lambda i: (0, 0)
    vec = pl.BlockSpec((None, 1, d), lambda i: (i // per, 0, 0))
    resident = lambda shape: pl.BlockSpec(shape, const, pipeline_mode=pl.Buffered(1))
    return pl.pallas_call(
        functools.partial(_mix_kernel, d=d),
        grid=(t // tm,),
        in_specs=[pl.BlockSpec((tm, z.shape[1]), row),
                  pl.BlockSpec((tm, on.shape[1]), row),
                  pl.BlockSpec((tm, 2 * d), row),
                  pl.BlockSpec((tm, d), row),
                  resident(wc.shape), resident(wh.shape), resident(wo.shape),
                  vec, pl.BlockSpec((1, d), const), vec, vec],
        out_specs=[pl.BlockSpec((tm, d), row), pl.BlockSpec((tm, d), row)],
        out_shape=[jax.ShapeDtypeStruct((t, d), F32), jax.ShapeDtypeStruct((t, d), BF16)],
        compiler_params=_cparams("parallel"),
        name="mix",
    )(z, on, sg, x, wc, wh, wo, gt, g2.reshape(1, d), sc, sh)


def _ffn_kernel(h_ref, wg_ref, wu_ref, wd_ref, x1_hbm, gt_ref, gf_ref, o_ref, x1_buf, x1_sem, *, tm):
    i = pl.program_id(0)
    j = pl.program_id(1)

    def x1_copy():
        return pltpu.make_async_copy(x1_hbm.at[pl.ds(i * tm, tm), :], x1_buf, x1_sem)

    halves = [slice(r0, r0 + tm // 2) for r0 in range(0, tm, tm // 2)]

    def down(rows):
        h = h_ref[rows, :]
        act = (_silu(_dot(h, wg_ref[...])) * _dot(h, wu_ref[...])).astype(BF16)
        return _dot(act, wd_ref[...])

    last = pl.num_programs(1) - 1

    @pl.when(j == 0)
    def _():
        x1_copy().start()
        for rows in halves:
            o_ref[rows, :] = down(rows)

    @pl.when(jnp.logical_and(j > 0, j < last))
    def _():
        for rows in halves:
            o_ref[rows, :] += down(rows)

    @pl.when(j == last)
    def _():
        x1_copy().wait()
        for rows in halves:
            x2 = x1_buf[rows, :] + gt_ref[...] * (o_ref[rows, :] + down(rows))
            o_ref[rows, :] = _rms(x2) * gf_ref[...]


def _ffn(h2, wg, wu, wd, x1, gt, gf, seq):
    t, d = x1.shape
    dff = wg.shape[1]
    tm, tf = TILES["ffn_tm"], TILES["ffn_tf"]
    per = seq // tm
    return pl.pallas_call(
        functools.partial(_ffn_kernel, tm=tm),
        grid=(t // tm, dff // tf),
        in_specs=[pl.BlockSpec((tm, d), lambda i, j: (i, 0)),
                  pl.BlockSpec((d, tf), lambda i, j: (0, j)),
                  pl.BlockSpec((d, tf), lambda i, j: (0, j)),
                  pl.BlockSpec((tf, d), lambda i, j: (j, 0)),
                  pl.BlockSpec(memory_space=pl.ANY),
                  pl.BlockSpec((None, 1, d), lambda i, j: (i // per, 0, 0)),
                  pl.BlockSpec((1, d), lambda i, j: (0, 0))],
        out_specs=pl.BlockSpec((tm, d), lambda i, j: (i, 0)),
        out_shape=jax.ShapeDtypeStruct((t, d), F32),
        scratch_shapes=[pltpu.VMEM((tm, d), F32), pltpu.SemaphoreType.DMA(())],
        compiler_params=_cparams("arbitrary", "arbitrary"),
        name="ffn",
    )(h2, wg, wu, wd, x1, gt, gf.reshape(1, d))


def kernel(x, c, w_ada, b_ada, norm_mix_g, w_in, conv_w, lb_param, gnorm_g, w_conv_out,
           w_hgrn_out, w_o, norm_ffn_g, w_ffn_gate, w_ffn_up, w_ffn_down, norm_final_g):
    b, s, d = x.shape
    t = b * s
    depth = w_in.shape[0]
    dc = conv_w.shape[2]
    dk = lb_param.shape[1]
    dv = w_hgrn_out.shape[1]
    assert dk == HEADS * HEAD_DIM and dv == HEADS * HEAD_DIM and s % CHUNK == 0
    tm, tm_conv, tm_norm, tm_gate = (TILES[key] for key in ("proj_tm", "conv_tm", "norm_tm", "gate_tm"))
    cw = TILES["conv_cw"]
    assert dc % cw == 0 and all(s % r == 0 and r % PROJ_PIECE == 0
                                for r in (tm, tm_conv, tm_norm, tm_gate))

    assert depth == 1
    l = 0

    mod = _modulation(c, w_ada[l], b_ada[l]).reshape(b, 6, 1, d)
    sh_m, sc_m, gt_m, sh_f, sc_f, gt_f = [mod[:, i] for i in range(6)]

    w = w_in[l]
    c_q = 3 * dc
    c_f, c_i, c_g, c_gate = c_q + dk, c_q + 2 * dk, c_q + 2 * dk + dv, c_q + 2 * dk + 2 * dv
    tg = TILES["gate_tn"]
    assert (w.shape[1] - c_gate) % tg == 0 and c_gate % tg == 0 and c_q % dk == 0 and dk == dv

    x2d = x.reshape(t, d)
    vec = pl.BlockSpec((None, 1, d), lambda j, i: (i // (s // tm_norm), 0, 0))
    q, h = _proj_call(
        _silu_epilogue, x2d, w, [c_q], dk, 1, [norm_mix_g[l].reshape(1, d), sc_m, sh_m],
        [pl.BlockSpec((1, d), lambda j, i: (0, 0)), vec, vec], [dk, d], [BF16, BF16], tm_norm,
        "proj_q", lhs=_norm_lhs)
    z, wd_b = _proj_call(
        functools.partial(_conv_epilogue, cw=cw, tiles_per_seq=s // tm_conv),
        h, w, [0, dc, 2 * dc], cw, dc // cw, [conv_w[l]],
        [pl.BlockSpec((CONV_WIDTH, cw), lambda j, i: (0, j))], [cw], [BF16], tm_conv, "proj_conv",
        scratch_shapes=[pltpu.VMEM((tm_conv + SUBLANES, cw), F32)], side_casts=[w_ffn_down[l]])
    kk, lf = _proj_call(
        functools.partial(_forget_epilogue, layer=l), h, w, [c_f], dk, 1, [lb_param],
        [pl.BlockSpec((lb_param.shape[0], dk), lambda j, i: (0, 0))],
        [dk, dk], [BF16, F32], tm, "proj_f")
    v, wc_b, wh_b = _proj_call(_ident_epilogue, h, w, [c_i], dv, 1, [], [], [dv], [BF16], tm,
                               "proj_i", side_casts=[w_conv_out[l], w_hgrn_out[l]])
    gs, wo_b = _proj_call(_gate_gain_epilogue, h, w, [c_g], dv, 1,
                          [jnp.tile(gnorm_g[l], HEADS).reshape(1, dv)],
                          [pl.BlockSpec((1, dv), lambda j, i: (0, 0))], [dv], [BF16], tm, "proj_g",
                          side_casts=[w_o[l]])
    sg, wg_b, wu_b = _proj_call(_sigmoid_epilogue, h, w, [c_gate], tg, (w.shape[1] - c_gate) // tg,
                                [], [], [tg], [BF16], tm_gate, "proj_gate",
                                side_casts=[w_ffn_gate[l], w_ffn_up[l]])

    on = _hgrn(q, kk, lf, v, gs, b, s)

    x1, h2 = _mix(z, on, sg, x2d, wc_b, wh_b, wo_b, gt_m, norm_ffn_g[l], sc_f, sh_f, s)
    out = _ffn(h2, wg_b, wu_b, wd_b, x1, gt_f, norm_final_g, s)
    return out.reshape(b, s, d)
```

```python
import functools

import jax
import jax.numpy as jnp
import numpy as np
from jax import lax
from jax.experimental import pallas as pl
from jax.experimental.pallas import tpu as pltpu

F32 = jnp.float32
BF16 = jnp.bfloat16

EPS = 1e-6
CONV_WIDTH = 3
HEADS = 8
HEAD_DIM = 128
CHUNK = 128
LEVELS = 8
FINE = 3
SUBLANES = 8
MIX_PIECE = 256
PROJ_PIECE = 128
VMEM_LIMIT = 56 * 1024 * 1024

TILES = dict(
    mod_tn=1024,
    proj_tm=2048,
    norm_tm=1024,
    conv_tm=1024,
    conv_cw=512,
    gate_tm=2048,
    gate_tn=1024,
    hgrn_rows=512,
    mix_tm=512,
    ffn_tm=1024,
    ffn_tf=512,
)


def _cparams(*sem):
    return pltpu.CompilerParams(dimension_semantics=sem, vmem_limit_bytes=VMEM_LIMIT)


def _dot(a, b):
    return jnp.dot(a, b, preferred_element_type=F32)


def _dot_nt(a, b):
    return lax.dot_general(a, b, (((1,), (1,)), ((), ())), preferred_element_type=F32)


def _silu(x):
    return x * jax.nn.sigmoid(x)


def _rms(x):
    return x * lax.rsqrt(jnp.mean(x * x, axis=-1, keepdims=True) + EPS)


def _mod_kernel(c_ref, w_ref, b_ref, o_ref):
    a = _silu(c_ref[...])
    nb = a.shape[0]
    a_hi = a.astype(BF16)
    a2 = jnp.concatenate([a_hi, (a - a_hi.astype(F32)).astype(BF16)], axis=0)
    w = w_ref[...]
    w_hi = w.astype(BF16)
    w_lo = (w - w_hi.astype(F32)).astype(BF16)
    acc = _dot(a2, w_hi) + _dot(a2, w_lo)
    o_ref[...] = acc[:nb] + acc[nb:] + b_ref[...]


def _modulation(c, w_ada, b_ada):
    b, d = c.shape
    n = w_ada.shape[1]
    tn = TILES["mod_tn"]
    return pl.pallas_call(
        _mod_kernel,
        grid=(n // tn,),
        in_specs=[pl.BlockSpec((b, d), lambda j: (0, 0)),
                  pl.BlockSpec((d, tn), lambda j: (0, j)),
                  pl.BlockSpec((1, tn), lambda j: (0, j))],
        out_specs=pl.BlockSpec((b, tn), lambda j: (0, j)),
        out_shape=jax.ShapeDtypeStruct((b, n), F32),
        compiler_params=_cparams("parallel"),
        name="mod",
    )(c, w_ada, b_ada.reshape(1, n))


def _rows_lhs(h_ref, rows, extra, outs):
    return h_ref[rows, :]


def _norm_lhs(x_ref, rows, extra, outs):
    g_ref, sc_ref, sh_ref = extra
    y = _rms(x_ref[rows, :]) * g_ref[...]
    h = (y * (1.0 + sc_ref[...]) + sh_ref[...]).astype(BF16)
    outs[-1][rows, :] = h
    return h


def _proj_kernel(*refs, n_w, n_extra, n_side, n_out, epilogue, lhs):
    src_ref = refs[0]
    pos = 1
    w_refs = refs[pos:pos + n_w]
    pos += n_w
    extra = refs[pos:pos + n_extra]
    pos += n_extra
    side_in = refs[pos:pos + n_side]
    pos += n_side
    outs = refs[pos:pos + n_out]
    pos += n_out
    side_out = refs[pos:pos + n_side]
    pos += n_side
    wbf = refs[pos]
    scratch = refs[pos + 1:]

    @pl.when(pl.program_id(1) == 0)
    def _():
        off = 0
        for w_ref in w_refs:
            wd = w_ref.shape[1]
            wbf[:, off:off + wd] = w_ref[...].astype(BF16)
            off += wd

    for s_ref, o_ref in zip(side_in, side_out):
        o_ref[...] = s_ref[...].astype(BF16)

    tm = src_ref.shape[0]
    for r0 in range(0, tm, PROJ_PIECE):
        rows = slice(r0, r0 + PROJ_PIECE)
        epilogue(_dot(lhs(src_ref, rows, extra, outs), wbf[...]), rows, tm, extra, outs, scratch)


def _proj_call(epilogue, h, w, col_starts, wd, nj, extras, extra_specs, out_widths, out_dtypes,
               tm, name, scratch_shapes=(), lhs=_rows_lhs, side_casts=()):
    t, d = h.shape
    ni = t // tm
    in_specs = [pl.BlockSpec((tm, d), lambda j, i: (i, 0))]
    for c0 in col_starts:
        in_specs.append(pl.BlockSpec((d, wd), functools.partial(lambda j, i, b: (0, b + j), b=c0 // wd)))
    in_specs += list(extra_specs)
    out_specs = [pl.BlockSpec((tm, ow), lambda j, i: (i, j)) for ow in out_widths]
    out_shape = [jax.ShapeDtypeStruct((t, nj * ow), dt) for ow, dt in zip(out_widths, out_dtypes)]
    for a in side_casts:
        rb = a.shape[0] // (nj * ni)
        assert rb * nj * ni == a.shape[0] and rb % (2 * SUBLANES) == 0
        spec = pl.BlockSpec((rb, a.shape[1]), lambda j, i: (j * ni + i, 0))
        in_specs.append(spec)
        out_specs.append(spec)
        out_shape.append(jax.ShapeDtypeStruct(a.shape, BF16))
    kern = functools.partial(_proj_kernel, n_w=len(col_starts), n_extra=len(extras),
                             n_side=len(side_casts), n_out=len(out_widths), epilogue=epilogue,
                             lhs=lhs)
    return pl.pallas_call(
        kern,
        grid=(nj, ni),
        in_specs=in_specs,
        out_specs=out_specs,
        out_shape=out_shape,
        scratch_shapes=[pltpu.VMEM((d, wd * len(col_starts)), BF16)] + list(scratch_shapes),
        compiler_params=_cparams("arbitrary", "arbitrary"),
        name=name,
    )(h, *([w] * len(col_starts)), *extras, *side_casts)


def _conv_epilogue(acc, rows, tm, extra, outs, scratch, *, cw, tiles_per_seq):
    cw_ref, = extra
    o_ref, = outs
    ubuf, = scratch
    r0, n = rows.start, rows.stop - rows.start
    u = acc[:, cw:2 * cw] * acc[:, 2 * cw:]

    if r0 == 0:
        @pl.when(pl.program_id(1) % tiles_per_seq == 0)
        def _():
            ubuf[0:SUBLANES, :] = jnp.zeros((SUBLANES, cw), F32)

    ubuf[SUBLANES + r0:SUBLANES + r0 + n, :] = u
    wk = cw_ref[...]
    y = wk[2:3, :] * u
    y = y + wk[1:2, :] * ubuf[SUBLANES - 1 + r0:SUBLANES - 1 + r0 + n, :]
    y = y + wk[0:1, :] * ubuf[SUBLANES - 2 + r0:SUBLANES - 2 + r0 + n, :]
    o_ref[rows, :] = (acc[:, :cw] * y).astype(BF16)
    if rows.stop == tm:
        ubuf[0:SUBLANES, :] = ubuf[tm:tm + SUBLANES, :]


def _silu_epilogue(acc, rows, tm, extra, outs, scratch):
    outs[0][rows, :] = _silu(acc).astype(BF16)


def _gate_gain_epilogue(acc, rows, tm, extra, outs, scratch):
    outs[0][rows, :] = (_silu(acc) * extra[0][...]).astype(BF16)


def _ident_epilogue(acc, rows, tm, extra, outs, scratch):
    outs[0][rows, :] = acc.astype(BF16)


def _sigmoid_epilogue(acc, rows, tm, extra, outs, scratch):
    outs[0][rows, :] = jax.nn.sigmoid(acc).astype(BF16)


def _forget_epilogue(acc, rows, tm, extra, outs, scratch, *, layer):
    k_ref, lf_ref = outs
    p = extra[0][...]
    e = jnp.exp(p - jnp.max(p, axis=0, keepdims=True))
    sm = e / jnp.sum(e, axis=0, keepdims=True)
    lb = jnp.sum(sm[:layer + 1, :], axis=0, keepdims=True)
    f = lb + (1.0 - lb) * jax.nn.sigmoid(acc)
    k_ref[rows, :] = (1.0 - f).astype(BF16)
    lf_ref[rows, :] = jnp.log2(f)


def _level_tables():
    n = CHUNK
    level = -np.ones((n, n), np.int32)
    expo = np.zeros((FINE + 1, n, n), np.float32)
    for t in range(n):
        level[t, t] = 0
        for s in range(t):
            level[t, s] = (t ^ s).bit_length()
        for li in range(1, FINE + 1):
            m = 1 << (li - 1)
            blk = (t // m) * m
            if (t % (2 * m)) >= m:
                expo[li - 1, t, blk:t + 1] = 1.0
            else:
                expo[li - 1, t, t + 1:blk + m] = 1.0
        expo[FINE, t, :t + 1] = 1.0
    expo = expo.reshape((FINE + 1) * n, n)
    return level, np.concatenate([expo, expo], axis=1)


def _hgrn_kernel(q_ref, k_ref, lf_ref, v_ref, gs_ref, lv_ref, ex_ref, o_ref, st_ref, dec_ref):
    @pl.when(pl.program_id(1) == 0)
    def _():
        st_ref[...] = jnp.zeros_like(st_ref)

    def chunk(ci, carry):
        rows = pl.ds(pl.multiple_of(ci * CHUNK, CHUNK), CHUNK)
        _hgrn_chunk(rows, q_ref, k_ref, lf_ref, v_ref, gs_ref, lv_ref, ex_ref, o_ref, st_ref, dec_ref)
        return carry

    lax.fori_loop(0, q_ref.shape[0] // CHUNK, chunk, 0)


def _hgrn_chunk(crows, q_ref, k_ref, lf_ref, v_ref, gs_ref, lv_ref, ex_ref, o_ref, st_ref, dec_ref):
    n = CHUNK
    nb = n // SUBLANES
    g0 = FINE * n
    tile = lambda a, r: a[r * SUBLANES:(r + 1) * SUBLANES, :]

    lf = lf_ref[crows, :]
    hi = lf.astype(BF16)
    lo = (lf - hi.astype(F32)).astype(BF16)
    dec_ref[...] = _dot(ex_ref[...], jnp.concatenate([hi, lo], axis=0))

    def pair_dot_nt(lhs2, rhs2):
        a0, a1 = lhs2
        z = jnp.zeros_like(a0)
        lhs = jnp.concatenate([jnp.concatenate([a0, z], axis=1),
                               jnp.concatenate([z, a1], axis=1)], axis=0)
        p = _dot_nt(lhs, jnp.concatenate(rhs2, axis=1))
        return p[:a0.shape[0]], p[a0.shape[0]:]

    for hd0 in range(0, HEADS, 2):
        pair = (hd0, hd0 + 1)
        sls = [slice(hd * HEAD_DIM, (hd + 1) * HEAD_DIM) for hd in pair]
        qb = [q_ref[crows, sl] for sl in sls]
        kb = [k_ref[crows, sl] for sl in sls]
        q = [a.astype(F32) for a in qb]
        k = [a.astype(F32) for a in kb]
        g = [dec_ref[g0:g0 + n, sl] for sl in sls]

        def halves_exponent(m, h):
            parts = []
            for r0 in range(0, n, 2 * m):
                g_n = dec_ref[g0 + r0 + m - 1:g0 + r0 + m, sls[h]]
                parts += [g_n - g[h][r0:r0 + m], g[h][r0 + m:r0 + 2 * m] - g_n]
            return jnp.concatenate(parts, axis=0)

        p2 = pair_dot_nt(qb, kb)
        sc = [[jnp.where(tile(lv_ref, r) == 0, tile(p, r), 0.0) for r in range(nb)] for p in p2]
        for li in range(1, LEVELS):
            m = 1 << (li - 1)
            lhs2, rhs2 = [], []
            for h in range(2):
                x = dec_ref[(li - 1) * n:li * n, sls[h]] if li <= FINE else halves_exponent(m, h)
                e = jnp.exp2(x)
                if m < SUBLANES:
                    rows = list(range(nb))
                    lhs2.append((q[h] * e).astype(BF16))
                    rhs2.append((k[h] * e).astype(BF16))
                else:
                    q_parts, k_parts, rows = [], [], []
                    for r0 in range(0, n, 2 * m):
                        k_parts += [k[h][r0:r0 + m] * e[r0:r0 + m], jnp.zeros((m, HEAD_DIM), F32)]
                        q_parts += [q[h][r0 + m:r0 + 2 * m] * e[r0 + m:r0 + 2 * m]]
                        rows += list(range((r0 + m) // SUBLANES, (r0 + 2 * m) // SUBLANES))
                    lhs2.append(jnp.concatenate(q_parts, axis=0).astype(BF16))
                    rhs2.append(jnp.concatenate(k_parts, axis=0).astype(BF16))
            p2 = pair_dot_nt(lhs2, rhs2)
            for h in range(2):
                for i, r in enumerate(rows):
                    sc[h][r] = jnp.where(tile(lv_ref, r) == li, tile(p2[h], i), sc[h][r])

        g_last = [dec_ref[g0 + n - 1:g0 + n, sl] for sl in sls]
        st = [st_ref[hd] for hd in pair]
        o_in = pair_dot_nt([(q[h] * jnp.exp2(g[h])).astype(BF16) for h in range(2)],
                           [a.astype(BF16) for a in st])
        for h in range(2):
            vb = v_ref[crows, sls[h]]
            o = o_in[h] + _dot(jnp.concatenate(sc[h], axis=0).astype(BF16), vb)
            k_out = (k[h] * jnp.exp2(g_last[h] - g[h])).astype(BF16)
            vt = vb.astype(F32).T.astype(BF16)
            st_ref[pair[h]] = st[h] * jnp.exp2(g_last[h]) + _dot(vt, k_out)
            o_ref[crows, sls[h]] = (_rms(o) * gs_ref[crows, sls[h]].astype(F32)).astype(BF16)


def _hgrn(q, k, lf, v, gs, batch, seq):
    t, dk = q.shape
    tr = TILES["hgrn_rows"]
    assert seq % tr == 0 and tr % CHUNK == 0
    nc = seq // tr
    level, expo = _level_tables()
    row = lambda b, c: (b * nc + c, 0)
    const = lambda b, c: (0, 0)
    blk = pl.BlockSpec((tr, dk), row)
    return pl.pallas_call(
        _hgrn_kernel,
        grid=(batch, nc),
        in_specs=[blk, blk, blk, blk, blk,
                  pl.BlockSpec((CHUNK, CHUNK), const),
                  pl.BlockSpec(expo.shape, const)],
        out_specs=blk,
        out_shape=jax.ShapeDtypeStruct((t, dk), BF16),
        scratch_shapes=[pltpu.VMEM((HEADS, HEAD_DIM, HEAD_DIM), F32),
                        pltpu.VMEM(((FINE + 1) * CHUNK, dk), F32)],
        compiler_params=_cparams("arbitrary", "arbitrary"),
        name="hgrn",
    )(q, k, lf, v, gs, jnp.asarray(level), jnp.asarray(expo, BF16))


def _mix_kernel(z_ref, on_ref, sg_ref, x_ref, wc_ref, wh_ref, wo_ref, gt_ref, g2_ref, sc_ref,
                sh_ref, x1_ref, h2_ref, *, d):
    for r0 in range(0, z_ref.shape[0], MIX_PIECE):
        rows = slice(r0, r0 + MIX_PIECE)
        y_a = _dot(z_ref[rows, :], wc_ref[...])
        y_b = _dot(on_ref[rows, :], wh_ref[...])
        merged = sg_ref[rows, :d].astype(F32) * y_a + sg_ref[rows, d:].astype(F32) * y_b
        x1 = x_ref[rows, :] + gt_ref[...] * _dot(merged.astype(BF16), wo_ref[...])
        x1_ref[rows, :] = x1
        h2_ref[rows, :] = (_rms(x1) * g2_ref[...] * (1.0 + sc_ref[...]) + sh_ref[...]).astype(BF16)


def _mix(z, on, sg, x, wc, wh, wo, gt, g2, sc, sh, seq):
    t, d = x.shape
    tm = TILES["mix_tm"]
    assert tm % MIX_PIECE == 0
    per = seq // tm
    row = lambda i: (i, 0)
    const = lambda i: (0, 0)
    vec = pl.BlockSpec((None, 1, d), lambda i: (i // per, 0, 0))
    resident = lambda shape: pl.BlockSpec(shape, const, pipeline_mode=pl.Buffered(1))
    return pl.pallas_call(
        functools.partial(_mix_kernel, d=d),
        grid=(t // tm,),
        in_specs=[pl.BlockSpec((tm, z.shape[1]), row),
                  pl.BlockSpec((tm, on.shape[1]), row),
                  pl.BlockSpec((tm, 2 * d), row),
                  pl.BlockSpec((tm, d), row),
                  resident(wc.shape), resident(wh.shape), resident(wo.shape),
                  vec, pl.BlockSpec((1, d), const), vec, vec],
        out_specs=[pl.BlockSpec((tm, d), row), pl.BlockSpec((tm, d), row)],
        out_shape=[jax.ShapeDtypeStruct((t, d), F32), jax.ShapeDtypeStruct((t, d), BF16)],
        compiler_params=_cparams("parallel"),
        name="mix",
    )(z, on, sg, x, wc, wh, wo, gt, g2.reshape(1, d), sc, sh)


def _ffn_kernel(h_ref, wg_ref, wu_ref, wd_ref, x1_hbm, gt_ref, gf_ref, o_ref, x1_buf, x1_sem, *, tm):
    i = pl.program_id(0)
    j = pl.program_id(1)

    def x1_copy():
        return pltpu.make_async_copy(x1_hbm.at[pl.ds(i * tm, tm), :], x1_buf, x1_sem)

    halves = [slice(r0, r0 + tm // 2) for r0 in range(0, tm, tm // 2)]

    def down(rows):
        h = h_ref[rows, :]
        act = (_silu(_dot(h, wg_ref[...])) * _dot(h, wu_ref[...])).astype(BF16)
        return _dot(act, wd_ref[...])

    last = pl.num_programs(1) - 1

    @pl.when(j == 0)
    def _():
        x1_copy().start()
        for rows in halves:
            o_ref[rows, :] = down(rows)

    @pl.when(jnp.logical_and(j > 0, j < last))
    def _():
        for rows in halves:
            o_ref[rows, :] += down(rows)

    @pl.when(j == last)
    def _():
        x1_copy().wait()
        for rows in halves:
            x2 = x1_buf[rows, :] + gt_ref[...] * (o_ref[rows, :] + down(rows))
            o_ref[rows, :] = _rms(x2) * gf_ref[...]


def _ffn(h2, wg, wu, wd, x1, gt, gf, seq):
    t, d = x1.shape
    dff = wg.shape[1]
    tm, tf = TILES["ffn_tm"], TILES["ffn_tf"]
    per = seq // tm
    return pl.pallas_call(
        functools.partial(_ffn_kernel, tm=tm),
        grid=(t // tm, dff // tf),
        in_specs=[pl.BlockSpec((tm, d), lambda i, j: (i, 0)),
                  pl.BlockSpec((d, tf), lambda i, j: (0, j)),
                  pl.BlockSpec((d, tf), lambda i, j: (0, j)),
                  pl.BlockSpec((tf, d), lambda i, j: (j, 0)),
                  pl.BlockSpec(memory_space=pl.ANY),
                  pl.BlockSpec((None, 1, d), lambda i, j: (i // per, 0, 0)),
                  pl.BlockSpec((1, d), lambda i, j: (0, 0))],
        out_specs=pl.BlockSpec((tm, d), lambda i, j: (i, 0)),
        out_shape=jax.ShapeDtypeStruct((t, d), F32),
        scratch_shapes=[pltpu.VMEM((tm, d), F32), pltpu.SemaphoreType.DMA(())],
        compiler_params=_cparams("arbitrary", "arbitrary"),
        name="ffn",
    )(h2, wg, wu, wd, x1, gt, gf.reshape(1, d))


def kernel(x, c, w_ada, b_ada, norm_mix_g, w_in, conv_w, lb_param, gnorm_g, w_conv_out,
           w_hgrn_out, w_o, norm_ffn_g, w_ffn_gate, w_ffn_up, w_ffn_down, norm_final_g):
    b, s, d = x.shape
    t = b * s
    depth = w_in.shape[0]
    dc = conv_w.shape[2]
    dk = lb_param.shape[1]
    dv = w_hgrn_out.shape[1]
    assert dk == HEADS * HEAD_DIM and dv == HEADS * HEAD_DIM and s % CHUNK == 0
    tm, tm_conv, tm_norm, tm_gate = (TILES[key] for key in ("proj_tm", "conv_tm", "norm_tm", "gate_tm"))
    cw = TILES["conv_cw"]
    assert dc % cw == 0 and all(s % r == 0 and r % PROJ_PIECE == 0
                                for r in (tm, tm_conv, tm_norm, tm_gate))

    assert depth == 1
    l = 0

    mod = _modulation(c, w_ada[l], b_ada[l]).reshape(b, 6, 1, d)
    sh_m, sc_m, gt_m, sh_f, sc_f, gt_f = [mod[:, i] for i in range(6)]

    w = w_in[l]
    c_q = 3 * dc
    c_f, c_i, c_g, c_gate = c_q + dk, c_q + 2 * dk, c_q + 2 * dk + dv, c_q + 2 * dk + 2 * dv
    tg = TILES["gate_tn"]
    assert (w.shape[1] - c_gate) % tg == 0 and c_gate % tg == 0 and c_q % dk == 0 and dk == dv

    x2d = x.reshape(t, d)
    vec = pl.BlockSpec((None, 1, d), lambda j, i: (i // (s // tm_norm), 0, 0))
    q, h = _proj_call(
        _silu_epilogue, x2d, w, [c_q], dk, 1, [norm_mix_g[l].reshape(1, d), sc_m, sh_m],
        [pl.BlockSpec((1, d), lambda j, i: (0, 0)), vec, vec], [dk, d], [BF16, BF16], tm_norm,
        "proj_q", lhs=_norm_lhs)
    z, wd_b = _proj_call(
        functools.partial(_conv_epilogue, cw=cw, tiles_per_seq=s // tm_conv),
        h, w, [0, dc, 2 * dc], cw, dc // cw, [conv_w[l]],
        [pl.BlockSpec((CONV_WIDTH, cw), lambda j, i: (0, j))], [cw], [BF16], tm_conv, "proj_conv",
        scratch_shapes=[pltpu.VMEM((tm_conv + SUBLANES, cw), F32)], side_casts=[w_ffn_down[l]])
    kk, lf = _proj_call(
        functools.partial(_forget_epilogue, layer=l), h, w, [c_f], dk, 1, [lb_param],
        [pl.BlockSpec((lb_param.shape[0], dk), lambda j, i: (0, 0))],
        [dk, dk], [BF16, F32], tm, "proj_f")
    v, wc_b, wh_b = _proj_call(_ident_epilogue, h, w, [c_i], dv, 1, [], [], [dv], [BF16], tm,
                               "proj_i", side_casts=[w_conv_out[l], w_hgrn_out[l]])
    gs, wo_b = _proj_call(_gate_gain_epilogue, h, w, [c_g], dv, 1,
                          [jnp.tile(gnorm_g[l], HEADS).reshape(1, dv)],
                          [pl.BlockSpec((1, dv), lambda j, i: (0, 0))], [dv], [BF16], tm, "proj_g",
                          side_casts=[w_o[l]])
    sg, wg_b, wu_b = _proj_call(_sigmoid_epilogue, h, w, [c_gate], tg, (w.shape[1] - c_gate) // tg,
                                [], [], [tg], [BF16], tm_gate, "proj_gate",
                                side_casts=[w_ffn_gate[l], w_ffn_up[l]])

    on = _hgrn(q, kk, lf, v, gs, b, s)

    x1, h2 = _mix(z, on, sg, x2d, wc_b, wh_b, wo_b, gt_m, norm_ffn_g[l], sc_f, sh_f, s)
    out = _ffn(h2, wg_b, wu_b, wd_b, x1, gt_f, norm_final_g, s)
    return out.reshape(b, s, d)
```

```python
import functools

import jax
import jax.numpy as jnp
import numpy as np
from jax import lax
from jax.experimental import pallas as pl
from jax.experimental.pallas import tpu as pltpu

F32 = jnp.float32
BF16 = jnp.bfloat16

EPS = 1e-6
CONV_WIDTH = 3
HEADS = 8
HEAD_DIM = 128
CHUNK = 128
LEVELS = 8
FINE = 3
SUBLANES = 8
MIX_PIECE = 256
PROJ_PIECE = 256
PROJ_PIECE_SMALL = 128
VMEM_LIMIT = 56 * 1024 * 1024

TILES = dict(
    mod_tn=1024,
    proj_tm=2048,
    norm_tm=1024,
    conv_tm=1024,
    conv_cw=512,
    gate_tm=2048,
    gate_tn=1024,
    hgrn_rows=512,
    mix_tm=512,
    ffn_tm=1024,
    ffn_tf=512,
)


def _cparams(*sem):
    return pltpu.CompilerParams(dimension_semantics=sem, vmem_limit_bytes=VMEM_LIMIT)


def _dot(a, b):
    return jnp.dot(a, b, preferred_element_type=F32)


def _dot_nt(a, b):
    return lax.dot_general(a, b, (((1,), (1,)), ((), ())), preferred_element_type=F32)


def _silu(x):
    return x * jax.nn.sigmoid(x)


def _rms(x):
    return x * lax.rsqrt(jnp.mean(x * x, axis=-1, keepdims=True) + EPS)


def _mod_kernel(c_ref, w_ref, b_ref, o_ref):
    a = _silu(c_ref[...])
    nb = a.shape[0]
    a_hi = a.astype(BF16)
    a2 = jnp.concatenate([a_hi, (a - a_hi.astype(F32)).astype(BF16)], axis=0)
    w = w_ref[...]
    w_hi = w.astype(BF16)
    w_lo = (w - w_hi.astype(F32)).astype(BF16)
    acc = _dot(a2, w_hi) + _dot(a2, w_lo)
    o_ref[...] = acc[:nb] + acc[nb:] + b_ref[...]


def _modulation(c, w_ada, b_ada):
    b, d = c.shape
    n = w_ada.shape[1]
    tn = TILES["mod_tn"]
    return pl.pallas_call(
        _mod_kernel,
        grid=(n // tn,),
        in_specs=[pl.BlockSpec((b, d), lambda j: (0, 0)),
                  pl.BlockSpec((d, tn), lambda j: (0, j)),
                  pl.BlockSpec((1, tn), lambda j: (0, j))],
        out_specs=pl.BlockSpec((b, tn), lambda j: (0, j)),
        out_shape=jax.ShapeDtypeStruct((b, n), F32),
        compiler_params=_cparams("parallel"),
        name="mod",
    )(c, w_ada, b_ada.reshape(1, n))


def _rows_lhs(h_ref, rows, extra, outs):
    return h_ref[rows, :]


def _norm_lhs(x_ref, rows, extra, outs):
    g_ref, sc_ref, sh_ref = extra
    y = _rms(x_ref[rows, :]) * g_ref[...]
    h = (y * (1.0 + sc_ref[...]) + sh_ref[...]).astype(BF16)
    outs[-1][rows, :] = h
    return h


def _proj_kernel(*refs, n_w, n_extra, n_side, n_out, epilogue, lhs, piece):
    src_ref = refs[0]
    pos = 1
    w_refs = refs[pos:pos + n_w]
    pos += n_w
    extra = refs[pos:pos + n_extra]
    pos += n_extra
    side_in = refs[pos:pos + n_side]
    pos += n_side
    outs = refs[pos:pos + n_out]
    pos += n_out
    side_out = refs[pos:pos + n_side]
    pos += n_side
    wbf = refs[pos]
    scratch = refs[pos + 1:]

    @pl.when(pl.program_id(1) == 0)
    def _():
        off = 0
        for w_ref in w_refs:
            wd = w_ref.shape[1]
            wbf[:, off:off + wd] = w_ref[...].astype(BF16)
            off += wd

    for s_ref, o_ref in zip(side_in, side_out):
        o_ref[...] = s_ref[...].astype(BF16)

    tm = src_ref.shape[0]
    for r0 in range(0, tm, piece):
        rows = slice(r0, r0 + piece)
        epilogue(_dot(lhs(src_ref, rows, extra, outs), wbf[...]), rows, tm, extra, outs, scratch)


def _proj_call(epilogue, h, w, col_starts, wd, nj, extras, extra_specs, out_widths, out_dtypes,
               tm, name, scratch_shapes=(), lhs=_rows_lhs, side_casts=(), piece=None):
    t, d = h.shape
    ni = t // tm
    in_specs = [pl.BlockSpec((tm, d), lambda j, i: (i, 0))]
    for c0 in col_starts:
        in_specs.append(pl.BlockSpec((d, wd), functools.partial(lambda j, i, b: (0, b + j), b=c0 // wd)))
    in_specs += list(extra_specs)
    out_specs = [pl.BlockSpec((tm, ow), lambda j, i: (i, j)) for ow in out_widths]
    out_shape = [jax.ShapeDtypeStruct((t, nj * ow), dt) for ow, dt in zip(out_widths, out_dtypes)]
    for a in side_casts:
        rb = a.shape[0] // (nj * ni)
        assert rb * nj * ni == a.shape[0] and rb % (2 * SUBLANES) == 0
        spec = pl.BlockSpec((rb, a.shape[1]), lambda j, i: (j * ni + i, 0))
        in_specs.append(spec)
        out_specs.append(spec)
        out_shape.append(jax.ShapeDtypeStruct(a.shape, BF16))
    kern = functools.partial(_proj_kernel, n_w=len(col_starts), n_extra=len(extras),
                             n_side=len(side_casts), n_out=len(out_widths), epilogue=epilogue,
                             lhs=lhs, piece=piece or PROJ_PIECE)
    return pl.pallas_call(
        kern,
        grid=(nj, ni),
        in_specs=in_specs,
        out_specs=out_specs,
        out_shape=out_shape,
        scratch_shapes=[pltpu.VMEM((d, wd * len(col_starts)), BF16)] + list(scratch_shapes),
        compiler_params=_cparams("arbitrary", "arbitrary"),
        name=name,
    )(h, *([w] * len(col_starts)), *extras, *side_casts)


def _conv_epilogue(acc, rows, tm, extra, outs, scratch, *, cw, tiles_per_seq):
    cw_ref, = extra
    o_ref, = outs
    ubuf, = scratch
    r0, n = rows.start, rows.stop - rows.start
    u = acc[:, cw:2 * cw] * acc[:, 2 * cw:]

    if r0 == 0:
        @pl.when(pl.program_id(1) % tiles_per_seq == 0)
        def _():
            ubuf[0:SUBLANES, :] = jnp.zeros((SUBLANES, cw), F32)

    ubuf[SUBLANES + r0:SUBLANES + r0 + n, :] = u
    wk = cw_ref[...]
    y = wk[2:3, :] * u
    y = y + wk[1:2, :] * ubuf[SUBLANES - 1 + r0:SUBLANES - 1 + r0 + n, :]
    y = y + wk[0:1, :] * ubuf[SUBLANES - 2 + r0:SUBLANES - 2 + r0 + n, :]
    o_ref[rows, :] = (acc[:, :cw] * y).astype(BF16)
    if rows.stop == tm:
        ubuf[0:SUBLANES, :] = ubuf[tm:tm + SUBLANES, :]


def _silu_epilogue(acc, rows, tm, extra, outs, scratch):
    outs[0][rows, :] = _silu(acc).astype(BF16)


def _gate_gain_epilogue(acc, rows, tm, extra, outs, scratch):
    outs[0][rows, :] = (_silu(acc) * extra[0][...]).astype(BF16)


def _ident_epilogue(acc, rows, tm, extra, outs, scratch):
    outs[0][rows, :] = acc.astype(BF16)


def _sigmoid_epilogue(acc, rows, tm, extra, outs, scratch):
    outs[0][rows, :] = jax.nn.sigmoid(acc).astype(BF16)


def _forget_epilogue(acc, rows, tm, extra, outs, scratch, *, layer):
    k_ref, lf_ref = outs
    p = extra[0][...]
    e = jnp.exp(p - jnp.max(p, axis=0, keepdims=True))
    sm = e / jnp.sum(e, axis=0, keepdims=True)
    lb = jnp.sum(sm[:layer + 1, :], axis=0, keepdims=True)
    f = lb + (1.0 - lb) * jax.nn.sigmoid(acc)
    k_ref[rows, :] = (1.0 - f).astype(BF16)
    lf_ref[rows, :] = jnp.log2(f)


def _level_tables():
    n = CHUNK
    level = -np.ones((n, n), np.int32)
    expo = np.zeros((FINE + 1, n, n), np.float32)
    for t in range(n):
        level[t, t] = 0
        for s in range(t):
            level[t, s] = (t ^ s).bit_length()
        for li in range(1, FINE + 1):
            m = 1 << (li - 1)
            blk = (t // m) * m
            if (t % (2 * m)) >= m:
                expo[li - 1, t, blk:t + 1] = 1.0
            else:
                expo[li - 1, t, t + 1:blk + m] = 1.0
        expo[FINE, t, :t + 1] = 1.0
    expo = expo.reshape((FINE + 1) * n, n)
    return level, np.concatenate([expo, expo], axis=1)


def _hgrn_kernel(q_ref, k_ref, lf_ref, v_ref, gs_ref, lv_ref, ex_ref, o_ref, st_ref, dec_ref):
    @pl.when(pl.program_id(1) == 0)
    def _():
        st_ref[...] = jnp.zeros_like(st_ref)

    def chunk(ci, carry):
        rows = pl.ds(pl.multiple_of(ci * CHUNK, CHUNK), CHUNK)
        _hgrn_chunk(rows, q_ref, k_ref, lf_ref, v_ref, gs_ref, lv_ref, ex_ref, o_ref, st_ref, dec_ref)
        return carry

    lax.fori_loop(0, q_ref.shape[0] // CHUNK, chunk, 0)


def _hgrn_chunk(crows, q_ref, k_ref, lf_ref, v_ref, gs_ref, lv_ref, ex_ref, o_ref, st_ref, dec_ref):
    n = CHUNK
    nb = n // SUBLANES
    g0 = FINE * n
    tile = lambda a, r: a[r * SUBLANES:(r + 1) * SUBLANES, :]

    lf = lf_ref[crows, :]
    hi = lf.astype(BF16)
    lo = (lf - hi.astype(F32)).astype(BF16)
    dec_ref[...] = _dot(ex_ref[...], jnp.concatenate([hi, lo], axis=0))

    def pair_dot_nt(lhs2, rhs2):
        a0, a1 = lhs2
        z = jnp.zeros_like(a0)
        lhs = jnp.concatenate([jnp.concatenate([a0, z], axis=1),
                               jnp.concatenate([z, a1], axis=1)], axis=0)
        p = _dot_nt(lhs, jnp.concatenate(rhs2, axis=1))
        return p[:a0.shape[0]], p[a0.shape[0]:]

    for hd0 in range(0, HEADS, 2):
        pair = (hd0, hd0 + 1)
        sls = [slice(hd * HEAD_DIM, (hd + 1) * HEAD_DIM) for hd in pair]
        qb = [q_ref[crows, sl] for sl in sls]
        kb = [k_ref[crows, sl] for sl in sls]
        q = [a.astype(F32) for a in qb]
        k = [a.astype(F32) for a in kb]
        g = [dec_ref[g0:g0 + n, sl] for sl in sls]

        def halves_exponent(m, h):
            parts = []
            for r0 in range(0, n, 2 * m):
                g_n = dec_ref[g0 + r0 + m - 1:g0 + r0 + m, sls[h]]
                parts += [g_n - g[h][r0:r0 + m], g[h][r0 + m:r0 + 2 * m] - g_n]
            return jnp.concatenate(parts, axis=0)

        p2 = pair_dot_nt(qb, kb)
        sc = [[jnp.where(tile(lv_ref, r) == 0, tile(p, r), 0.0) for r in range(nb)] for p in p2]
        for li in range(1, LEVELS):
            m = 1 << (li - 1)
            lhs2, rhs2 = [], []
            for h in range(2):
                x = dec_ref[(li - 1) * n:li * n, sls[h]] if li <= FINE else halves_exponent(m, h)
                e = jnp.exp2(x)
                if m < SUBLANES:
                    rows = list(range(nb))
                    lhs2.append((q[h] * e).astype(BF16))
                    rhs2.append((k[h] * e).astype(BF16))
                else:
                    q_parts, k_parts, rows = [], [], []
                    for r0 in range(0, n, 2 * m):
                        k_parts += [k[h][r0:r0 + m] * e[r0:r0 + m], jnp.zeros((m, HEAD_DIM), F32)]
                        q_parts += [q[h][r0 + m:r0 + 2 * m] * e[r0 + m:r0 + 2 * m]]
                        rows += list(range((r0 + m) // SUBLANES, (r0 + 2 * m) // SUBLANES))
                    lhs2.append(jnp.concatenate(q_parts, axis=0).astype(BF16))
                    rhs2.append(jnp.concatenate(k_parts, axis=0).astype(BF16))
            p2 = pair_dot_nt(lhs2, rhs2)
            for h in range(2):
                for i, r in enumerate(rows):
                    sc[h][r] = jnp.where(tile(lv_ref, r) == li, tile(p2[h], i), sc[h][r])

        g_last = [dec_ref[g0 + n - 1:g0 + n, sl] for sl in sls]
        st = [st_ref[hd] for hd in pair]
        o_in = pair_dot_nt([(q[h] * jnp.exp2(g[h])).astype(BF16) for h in range(2)],
                           [a.astype(BF16) for a in st])
        for h in range(2):
            vb = v_ref[crows, sls[h]]
            o = o_in[h] + _dot(jnp.concatenate(sc[h], axis=0).astype(BF16), vb)
            k_out = (k[h] * jnp.exp2(g_last[h] - g[h])).astype(BF16)
            vt = vb.astype(F32).T.astype(BF16)
            st_ref[pair[h]] = st[h] * jnp.exp2(g_last[h]) + _dot(vt, k_out)
            o_ref[crows, sls[h]] = (_rms(o) * gs_ref[crows, sls[h]].astype(F32)).astype(BF16)


def _hgrn(q, k, lf, v, gs, batch, seq):
    t, dk = q.shape
    tr = TILES["hgrn_rows"]
    assert seq % tr == 0 and tr % CHUNK == 0
    nc = seq // tr
    level, expo = _level_tables()
    row = lambda b, c: (b * nc + c, 0)
    const = lambda b, c: (0, 0)
    blk = pl.BlockSpec((tr, dk), row)
    return pl.pallas_call(
        _hgrn_kernel,
        grid=(batch, nc),
        in_specs=[blk, blk, blk, blk, blk,
                  pl.BlockSpec((CHUNK, CHUNK), const),
                  pl.BlockSpec(expo.shape, const)],
        out_specs=blk,
        out_shape=jax.ShapeDtypeStruct((t, dk), BF16),
        scratch_shapes=[pltpu.VMEM((HEADS, HEAD_DIM, HEAD_DIM), F32),
                        pltpu.VMEM(((FINE + 1) * CHUNK, dk), F32)],
        compiler_params=_cparams("arbitrary", "arbitrary"),
        name="hgrn",
    )(q, k, lf, v, gs, jnp.asarray(level), jnp.asarray(expo, BF16))


def _mix_kernel(z_ref, on_ref, sg_ref, x_ref, wc_ref, wh_ref, wo_ref, gt_ref, g2_ref, sc_ref,
                sh_ref, x1_ref, h2_ref, *, d):
    for r0 in range(0, z_ref.shape[0], MIX_PIECE):
        rows = slice(r0, r0 + MIX_PIECE)
        y_a = _dot(z_ref[rows, :], wc_ref[...])
        y_b = _dot(on_ref[rows, :], wh_ref[...])
        merged = sg_ref[rows, :d].astype(F32) * y_a + sg_ref[rows, d:].astype(F32) * y_b
        x1 = x_ref[rows, :] + gt_ref[...] * _dot(merged.astype(BF16), wo_ref[...])
        x1_ref[rows, :] = x1
        h2_ref[rows, :] = (_rms(x1) * g2_ref[...] * (1.0 + sc_ref[...]) + sh_ref[...]).astype(BF16)


def _mix(z, on, sg, x, wc, wh, wo, gt, g2, sc, sh, seq):
    t, d = x.shape
    tm = TILES["mix_tm"]
    assert tm % MIX_PIECE == 0
    per = seq // tm
    row = lambda i: (i, 0)
    const = lambda i: (0, 0)
    vec = pl.BlockSpec((None, 1, d), lambda i: (i // per, 0, 0))
    resident = lambda shape: pl.BlockSpec(shape, const, pipeline_mode=pl.Buffered(1))
    return pl.pallas_call(
        functools.partial(_mix_kernel, d=d),
        grid=(t // tm,),
        in_specs=[pl.BlockSpec((tm, z.shape[1]), row),
                  pl.BlockSpec((tm, on.shape[1]), row),
                  pl.BlockSpec((tm, 2 * d), row),
                  pl.BlockSpec((tm, d), row),
                  resident(wc.shape), resident(wh.shape), resident(wo.shape),
                  vec, pl.BlockSpec((1, d), const), vec, vec],
        out_specs=[pl.BlockSpec((tm, d), row), pl.BlockSpec((tm, d), row)],
        out_shape=[jax.ShapeDtypeStruct((t, d), F32), jax.ShapeDtypeStruct((t, d), BF16)],
        compiler_params=_cparams("parallel"),
        name="mix",
    )(z, on, sg, x, wc, wh, wo, gt, g2.reshape(1, d), sc, sh)


def _ffn_kernel(h_ref, wg_ref, wu_ref, wd_ref, x1_hbm, gt_ref, gf_ref, o_ref, x1_buf, x1_sem, *, tm):
    i = pl.program_id(0)
    j = pl.program_id(1)

    def x1_copy():
        return pltpu.make_async_copy(x1_hbm.at[pl.ds(i * tm, tm), :], x1_buf, x1_sem)

    halves = [slice(r0, r0 + tm // 2) for r0 in range(0, tm, tm // 2)]

    def down(rows):
        h = h_ref[rows, :]
        act = (_silu(_dot(h, wg_ref[...])) * _dot(h, wu_ref[...])).astype(BF16)
        return _dot(act, wd_ref[...])

    last = pl.num_programs(1) - 1

    @pl.when(j == 0)
    def _():
        x1_copy().start()
        for rows in halves:
            o_ref[rows, :] = down(rows)

    @pl.when(jnp.logical_and(j > 0, j < last))
    def _():
        for rows in halves:
            o_ref[rows, :] += down(rows)

    @pl.when(j == last)
    def _():
        x1_copy().wait()
        for rows in halves:
            x2 = x1_buf[rows, :] + gt_ref[...] * (o_ref[rows, :] + down(rows))
            o_ref[rows, :] = _rms(x2) * gf_ref[...]


def _ffn(h2, wg, wu, wd, x1, gt, gf, seq):
    t, d = x1.shape
    dff = wg.shape[1]
    tm, tf = TILES["ffn_tm"], TILES["ffn_tf"]
    per = seq // tm
    return pl.pallas_call(
        functools.partial(_ffn_kernel, tm=tm),
        grid=(t // tm, dff // tf),
        in_specs=[pl.BlockSpec((tm, d), lambda i, j: (i, 0)),
                  pl.BlockSpec((d, tf), lambda i, j: (0, j)),
                  pl.BlockSpec((d, tf), lambda i, j: (0, j)),
                  pl.BlockSpec((tf, d), lambda i, j: (j, 0)),
                  pl.BlockSpec(memory_space=pl.ANY),
                  pl.BlockSpec((None, 1, d), lambda i, j: (i // per, 0, 0)),
                  pl.BlockSpec((1, d), lambda i, j: (0, 0))],
        out_specs=pl.BlockSpec((tm, d), lambda i, j: (i, 0)),
        out_shape=jax.ShapeDtypeStruct((t, d), F32),
        scratch_shapes=[pltpu.VMEM((tm, d), F32), pltpu.SemaphoreType.DMA(())],
        compiler_params=_cparams("arbitrary", "arbitrary"),
        name="ffn",
    )(h2, wg, wu, wd, x1, gt, gf.reshape(1, d))


def kernel(x, c, w_ada, b_ada, norm_mix_g, w_in, conv_w, lb_param, gnorm_g, w_conv_out,
           w_hgrn_out, w_o, norm_ffn_g, w_ffn_gate, w_ffn_up, w_ffn_down, norm_final_g):
    b, s, d = x.shape
    t = b * s
    depth = w_in.shape[0]
    dc = conv_w.shape[2]
    dk = lb_param.shape[1]
    dv = w_hgrn_out.shape[1]
    assert dk == HEADS * HEAD_DIM and dv == HEADS * HEAD_DIM and s % CHUNK == 0
    tm, tm_conv, tm_norm, tm_gate = (TILES[key] for key in ("proj_tm", "conv_tm", "norm_tm", "gate_tm"))
    cw = TILES["conv_cw"]
    assert dc % cw == 0 and all(s % r == 0 and r % PROJ_PIECE == 0
                                for r in (tm, tm_conv, tm_norm, tm_gate))

    assert depth == 1
    l = 0

    mod = _modulation(c, w_ada[l], b_ada[l]).reshape(b, 6, 1, d)
    sh_m, sc_m, gt_m, sh_f, sc_f, gt_f = [mod[:, i] for i in range(6)]

    w = w_in[l]
    c_q = 3 * dc
    c_f, c_i, c_g, c_gate = c_q + dk, c_q + 2 * dk, c_q + 2 * dk + dv, c_q + 2 * dk + 2 * dv
    tg = TILES["gate_tn"]
    assert (w.shape[1] - c_gate) % tg == 0 and c_gate % tg == 0 and c_q % dk == 0 and dk == dv

    x2d = x.reshape(t, d)
    vec = pl.BlockSpec((None, 1, d), lambda j, i: (i // (s // tm_norm), 0, 0))
    q, h = _proj_call(
        _silu_epilogue, x2d, w, [c_q], dk, 1, [norm_mix_g[l].reshape(1, d), sc_m, sh_m],
        [pl.BlockSpec((1, d), lambda j, i: (0, 0)), vec, vec], [dk, d], [BF16, BF16], tm_norm,
        "proj_q", lhs=_norm_lhs)
    z, wd_b = _proj_call(
        functools.partial(_conv_epilogue, cw=cw, tiles_per_seq=s // tm_conv),
        h, w, [0, dc, 2 * dc], cw, dc // cw, [conv_w[l]],
        [pl.BlockSpec((CONV_WIDTH, cw), lambda j, i: (0, j))], [cw], [BF16], tm_conv, "proj_conv",
        scratch_shapes=[pltpu.VMEM((tm_conv + SUBLANES, cw), F32)], side_casts=[w_ffn_down[l]])
    kk, lf = _proj_call(
        functools.partial(_forget_epilogue, layer=l), h, w, [c_f], dk, 1, [lb_param],
        [pl.BlockSpec((lb_param.shape[0], dk), lambda j, i: (0, 0))],
        [dk, dk], [BF16, F32], tm, "proj_f", piece=PROJ_PIECE_SMALL)
    v, wc_b, wh_b = _proj_call(_ident_epilogue, h, w, [c_i], dv, 1, [], [], [dv], [BF16], tm,
                               "proj_i", side_casts=[w_conv_out[l], w_hgrn_out[l]])
    gs, wo_b = _proj_call(_gate_gain_epilogue, h, w, [c_g], dv, 1,
                          [jnp.tile(gnorm_g[l], HEADS).reshape(1, dv)],
                          [pl.BlockSpec((1, dv), lambda j, i: (0, 0))], [dv], [BF16], tm, "proj_g",
                          side_casts=[w_o[l]])
    sg, wg_b, wu_b = _proj_call(_sigmoid_epilogue, h, w, [c_gate], tg, (w.shape[1] - c_gate) // tg,
                                [], [], [tg], [BF16], tm_gate, "proj_gate",
                                side_casts=[w_ffn_gate[l], w_ffn_up[l]], piece=PROJ_PIECE_SMALL)

    on = _hgrn(q, kk, lf, v, gs, b, s)

    x1, h2 = _mix(z, on, sg, x2d, wc_b, wh_b, wo_b, gt_m, norm_ffn_g[l], sc_f, sh_f, s)
    out = _ffn(h2, wg_b, wu_b, wd_b, x1, gt_f, norm_final_g, s)
    return out.reshape(b, s, d)
```

```python
import functools

import jax
import jax.numpy as jnp
import numpy as np
from jax import lax
from jax.experimental import pallas as pl
from jax.experimental.pallas import tpu as pltpu

F32 = jnp.float32
BF16 = jnp.bfloat16

EPS = 1e-6
CONV_WIDTH = 3
HEADS = 8
HEAD_DIM = 128
CHUNK = 128
LEVELS = 8
FINE = 3
SUBLANES = 8
MOD_BUFFERS = 3
MIX_PIECE = 256
PROJ_PIECE = 256
VMEM_LIMIT = 56 * 1024 * 1024

TILES = dict(
    mod_tn=1024,
    proj_tm=2048,
    norm_tm=1024,
    conv_tm=1024,
    conv_cw=512,
    gate_tm=2048,
    gate_tn=1024,
    hgrn_rows=512,
    mix_tm=512,
    ffn_tm=1024,
    ffn_tf=512,
)


def _cparams(*sem):
    return pltpu.CompilerParams(dimension_semantics=sem, vmem_limit_bytes=VMEM_LIMIT)


def _dot(a, b):
    return jnp.dot(a, b, preferred_element_type=F32)


def _dot_nt(a, b):
    return lax.dot_general(a, b, (((1,), (1,)), ((), ())), preferred_element_type=F32)


def _silu(x):
    return x * jax.nn.sigmoid(x)


def _rms(x):
    return x * lax.rsqrt(jnp.mean(x * x, axis=-1, keepdims=True) + EPS)


def _mod_kernel(c_ref, w_hbm, b_ref, o_ref, wbuf, sem, *, tn, steps):
    j = pl.program_id(0)

    def tile_copy(t, slot):
        cols = pl.ds(pl.multiple_of(t * tn, tn), tn)
        return pltpu.make_async_copy(w_hbm.at[:, cols], wbuf.at[slot], sem.at[slot])

    @pl.when(j == 0)
    def _():
        for t in range(min(MOD_BUFFERS, steps)):
            tile_copy(t, t).start()

    slot = j % MOD_BUFFERS
    tile_copy(j, slot).wait()

    a = _silu(c_ref[...])
    nb = a.shape[0]
    a_hi = a.astype(BF16)
    a2 = jnp.concatenate([a_hi, (a - a_hi.astype(F32)).astype(BF16)], axis=0)
    w = wbuf[slot]
    w_hi = w.astype(BF16)
    w_lo = (w - w_hi.astype(F32)).astype(BF16)
    acc = _dot(a2, w_hi) + _dot(a2, w_lo)
    o_ref[...] = acc[:nb] + acc[nb:] + b_ref[...]

    @pl.when(j + MOD_BUFFERS < steps)
    def _():
        tile_copy(j + MOD_BUFFERS, slot).start()


def _modulation(c, w_ada, b_ada):
    b, d = c.shape
    n = w_ada.shape[1]
    tn = TILES["mod_tn"]
    steps = n // tn
    return pl.pallas_call(
        functools.partial(_mod_kernel, tn=tn, steps=steps),
        grid=(steps,),
        in_specs=[pl.BlockSpec((b, d), lambda j: (0, 0)),
                  pl.BlockSpec(memory_space=pl.ANY),
                  pl.BlockSpec((1, tn), lambda j: (0, j))],
        out_specs=pl.BlockSpec((b, tn), lambda j: (0, j)),
        out_shape=jax.ShapeDtypeStruct((b, n), F32),
        scratch_shapes=[pltpu.VMEM((MOD_BUFFERS, d, tn), F32),
                        pltpu.SemaphoreType.DMA((MOD_BUFFERS,))],
        compiler_params=_cparams("arbitrary"),
        name="mod",
    )(c, w_ada, b_ada.reshape(1, n))


def _rows_lhs(h_ref, rows, extra, outs):
    return h_ref[rows, :]


def _norm_lhs(x_ref, rows, extra, outs):
    g_ref, sc_ref, sh_ref = extra
    y = _rms(x_ref[rows, :]) * g_ref[...]
    h = (y * (1.0 + sc_ref[...]) + sh_ref[...]).astype(BF16)
    outs[-1][rows, :] = h
    return h


def _proj_kernel(*refs, n_w, n_extra, n_side, n_out, epilogue, lhs):
    src_ref = refs[0]
    pos = 1
    w_refs = refs[pos:pos + n_w]
    pos += n_w
    extra = refs[pos:pos + n_extra]
    pos += n_extra
    side_in = refs[pos:pos + n_side]
    pos += n_side
    outs = refs[pos:pos + n_out]
    pos += n_out
    side_out = refs[pos:pos + n_side]
    pos += n_side
    wbf = refs[pos]
    scratch = refs[pos + 1:]

    @pl.when(pl.program_id(1) == 0)
    def _():
        off = 0
        for w_ref in w_refs:
            wd = w_ref.shape[1]
            wbf[:, off:off + wd] = w_ref[...].astype(BF16)
            off += wd

    for s_ref, o_ref in zip(side_in, side_out):
        o_ref[...] = s_ref[...].astype(BF16)

    tm = src_ref.shape[0]
    for r0 in range(0, tm, PROJ_PIECE):
        rows = slice(r0, r0 + PROJ_PIECE)
        epilogue(_dot(lhs(src_ref, rows, extra, outs), wbf[...]), rows, tm, extra, outs, scratch)


def _proj_call(epilogue, h, w, col_starts, wd, nj, extras, extra_specs, out_widths, out_dtypes,
               tm, name, scratch_shapes=(), lhs=_rows_lhs, side_casts=()):
    t, d = h.shape
    ni = t // tm
    in_specs = [pl.BlockSpec((tm, d), lambda j, i: (i, 0))]
    for c0 in col_starts:
        in_specs.append(pl.BlockSpec((d, wd), functools.partial(lambda j, i, b: (0, b + j), b=c0 // wd)))
    in_specs += list(extra_specs)
    out_specs = [pl.BlockSpec((tm, ow), lambda j, i: (i, j)) for ow in out_widths]
    out_shape = [jax.ShapeDtypeStruct((t, nj * ow), dt) for ow, dt in zip(out_widths, out_dtypes)]
    for a in side_casts:
        rb = a.shape[0] // (nj * ni)
        assert rb * nj * ni == a.shape[0] and rb % (2 * SUBLANES) == 0
        spec = pl.BlockSpec((rb, a.shape[1]), lambda j, i: (j * ni + i, 0))
        in_specs.append(spec)
        out_specs.append(spec)
        out_shape.append(jax.ShapeDtypeStruct(a.shape, BF16))
    kern = functools.partial(_proj_kernel, n_w=len(col_starts), n_extra=len(extras),
                             n_side=len(side_casts), n_out=len(out_widths), epilogue=epilogue,
                             lhs=lhs)
    return pl.pallas_call(
        kern,
        grid=(nj, ni),
        in_specs=in_specs,
        out_specs=out_specs,
        out_shape=out_shape,
        scratch_shapes=[pltpu.VMEM((d, wd * len(col_starts)), BF16)] + list(scratch_shapes),
        compiler_params=_cparams("arbitrary", "arbitrary"),
        name=name,
    )(h, *([w] * len(col_starts)), *extras, *side_casts)


def _conv_epilogue(acc, rows, tm, extra, outs, scratch, *, cw, tiles_per_seq):
    cw_ref, = extra
    o_ref, = outs
    ubuf, = scratch
    r0, n = rows.start, rows.stop - rows.start
    u = acc[:, cw:2 * cw] * acc[:, 2 * cw:]

    if r0 == 0:
        @pl.when(pl.program_id(1) % tiles_per_seq == 0)
        def _():
            ubuf[0:SUBLANES, :] = jnp.zeros((SUBLANES, cw), F32)

    ubuf[SUBLANES + r0:SUBLANES + r0 + n, :] = u
    wk = cw_ref[...]
    y = wk[2:3, :] * u
    y = y + wk[1:2, :] * ubuf[SUBLANES - 1 + r0:SUBLANES - 1 + r0 + n, :]
    y = y + wk[0:1, :] * ubuf[SUBLANES - 2 + r0:SUBLANES - 2 + r0 + n, :]
    o_ref[rows, :] = (acc[:, :cw] * y).astype(BF16)
    if rows.stop == tm:
        ubuf[0:SUBLANES, :] = ubuf[tm:tm + SUBLANES, :]


def _silu_epilogue(acc, rows, tm, extra, outs, scratch):
    outs[0][rows, :] = _silu(acc).astype(BF16)


def _gate_gain_epilogue(acc, rows, tm, extra, outs, scratch):
    outs[0][rows, :] = (_silu(acc) * extra[0][...]).astype(BF16)


def _ident_epilogue(acc, rows, tm, extra, outs, scratch):
    outs[0][rows, :] = acc.astype(BF16)


def _sigmoid_epilogue(acc, rows, tm, extra, outs, scratch):
    outs[0][rows, :] = jax.nn.sigmoid(acc).astype(BF16)


def _forget_epilogue(acc, rows, tm, extra, outs, scratch, *, layer):
    k_ref, lf_ref = outs
    p = extra[0][...]
    e = jnp.exp(p - jnp.max(p, axis=0, keepdims=True))
    sm = e / jnp.sum(e, axis=0, keepdims=True)
    lb = jnp.sum(sm[:layer + 1, :], axis=0, keepdims=True)
    f = lb + (1.0 - lb) * jax.nn.sigmoid(acc)
    k_ref[rows, :] = (1.0 - f).astype(BF16)
    lf_ref[rows, :] = jnp.log2(f)


def _level_tables():
    n = CHUNK
    level = -np.ones((n, n), np.int32)
    expo = np.zeros((FINE + 1, n, n), np.float32)
    for t in range(n):
        level[t, t] = 0
        for s in range(t):
            level[t, s] = (t ^ s).bit_length()
        for li in range(1, FINE + 1):
            m = 1 << (li - 1)
            blk = (t // m) * m
            if (t % (2 * m)) >= m:
                expo[li - 1, t, blk:t + 1] = 1.0
            else:
                expo[li - 1, t, t + 1:blk + m] = 1.0
        expo[FINE, t, :t + 1] = 1.0
    expo = expo.reshape((FINE + 1) * n, n)
    return level, np.concatenate([expo, expo], axis=1)


def _hgrn_kernel(q_ref, k_ref, lf_ref, v_ref, gs_ref, lv_ref, ex_ref, o_ref, st_ref, dec_ref):
    @pl.when(pl.program_id(1) == 0)
    def _():
        st_ref[...] = jnp.zeros_like(st_ref)

    def chunk(ci, carry):
        rows = pl.ds(pl.multiple_of(ci * CHUNK, CHUNK), CHUNK)
        _hgrn_chunk(rows, q_ref, k_ref, lf_ref, v_ref, gs_ref, lv_ref, ex_ref, o_ref, st_ref, dec_ref)
        return carry

    lax.fori_loop(0, q_ref.shape[0] // CHUNK, chunk, 0)


def _hgrn_chunk(crows, q_ref, k_ref, lf_ref, v_ref, gs_ref, lv_ref, ex_ref, o_ref, st_ref, dec_ref):
    n = CHUNK
    nb = n // SUBLANES
    g0 = FINE * n
    tile = lambda a, r: a[r * SUBLANES:(r + 1) * SUBLANES, :]

    lf = lf_ref[crows, :]
    hi = lf.astype(BF16)
    lo = (lf - hi.astype(F32)).astype(BF16)
    dec_ref[...] = _dot(ex_ref[...], jnp.concatenate([hi, lo], axis=0))

    def pair_dot_nt(lhs2, rhs2):
        a0, a1 = lhs2
        z = jnp.zeros_like(a0)
        lhs = jnp.concatenate([jnp.concatenate([a0, z], axis=1),
                               jnp.concatenate([z, a1], axis=1)], axis=0)
        p = _dot_nt(lhs, jnp.concatenate(rhs2, axis=1))
        return p[:a0.shape[0]], p[a0.shape[0]:]

    for hd0 in range(0, HEADS, 2):
        pair = (hd0, hd0 + 1)
        sls = [slice(hd * HEAD_DIM, (hd + 1) * HEAD_DIM) for hd in pair]
        qb = [q_ref[crows, sl] for sl in sls]
        kb = [k_ref[crows, sl] for sl in sls]
        q = [a.astype(F32) for a in qb]
        k = [a.astype(F32) for a in kb]
        g = [dec_ref[g0:g0 + n, sl] for sl in sls]

        def halves_exponent(m, h):
            parts = []
            for r0 in range(0, n, 2 * m):
                g_n = dec_ref[g0 + r0 + m - 1:g0 + r0 + m, sls[h]]
                parts += [g_n - g[h][r0:r0 + m], g[h][r0 + m:r0 + 2 * m] - g_n]
            return jnp.concatenate(parts, axis=0)

        p2 = pair_dot_nt(qb, kb)
        sc = [[jnp.where(tile(lv_ref, r) == 0, tile(p, r), 0.0) for r in range(nb)] for p in p2]
        for li in range(1, LEVELS):
            m = 1 << (li - 1)
            lhs2, rhs2 = [], []
            for h in range(2):
                x = dec_ref[(li - 1) * n:li * n, sls[h]] if li <= FINE else halves_exponent(m, h)
                e = jnp.exp2(x)
                if m < SUBLANES:
                    rows = list(range(nb))
                    lhs2.append((q[h] * e).astype(BF16))
                    rhs2.append((k[h] * e).astype(BF16))
                else:
                    q_parts, k_parts, rows = [], [], []
                    for r0 in range(0, n, 2 * m):
                        k_parts += [k[h][r0:r0 + m] * e[r0:r0 + m], jnp.zeros((m, HEAD_DIM), F32)]
                        q_parts += [q[h][r0 + m:r0 + 2 * m] * e[r0 + m:r0 + 2 * m]]
                        rows += list(range((r0 + m) // SUBLANES, (r0 + 2 * m) // SUBLANES))
                    lhs2.append(jnp.concatenate(q_parts, axis=0).astype(BF16))
                    rhs2.append(jnp.concatenate(k_parts, axis=0).astype(BF16))
            p2 = pair_dot_nt(lhs2, rhs2)
            for h in range(2):
                for i, r in enumerate(rows):
                    sc[h][r] = jnp.where(tile(lv_ref, r) == li, tile(p2[h], i), sc[h][r])

        g_last = [dec_ref[g0 + n - 1:g0 + n, sl] for sl in sls]
        st = [st_ref[hd] for hd in pair]
        o_in = pair_dot_nt([(q[h] * jnp.exp2(g[h])).astype(BF16) for h in range(2)],
                           [a.astype(BF16) for a in st])
        for h in range(2):
            vb = v_ref[crows, sls[h]]
            o = o_in[h] + _dot(jnp.concatenate(sc[h], axis=0).astype(BF16), vb)
            k_out = (k[h] * jnp.exp2(g_last[h] - g[h])).astype(BF16)
            vt = vb.astype(F32).T.astype(BF16)
            st_ref[pair[h]] = st[h] * jnp.exp2(g_last[h]) + _dot(vt, k_out)
            o_ref[crows, sls[h]] = (_rms(o) * gs_ref[crows, sls[h]].astype(F32)).astype(BF16)


def _hgrn(q, k, lf, v, gs, batch, seq):
    t, dk = q.shape
    tr = TILES["hgrn_rows"]
    assert seq % tr == 0 and tr % CHUNK == 0
    nc = seq // tr
    level, expo = _level_tables()
    row = lambda b, c: (b * nc + c, 0)
    const = lambda b, c: (0, 0)
    blk = pl.BlockSpec((tr, dk), row)
    return pl.pallas_call(
        _hgrn_kernel,
        grid=(batch, nc),
        in_specs=[blk, blk, blk, blk, blk,
                  pl.BlockSpec((CHUNK, CHUNK), const),
                  pl.BlockSpec(expo.shape, const)],
        out_specs=blk,
        out_shape=jax.ShapeDtypeStruct((t, dk), BF16),
        scratch_shapes=[pltpu.VMEM((HEADS, HEAD_DIM, HEAD_DIM), F32),
                        pltpu.VMEM(((FINE + 1) * CHUNK, dk), F32)],
        compiler_params=_cparams("arbitrary", "arbitrary"),
        name="hgrn",
    )(q, k, lf, v, gs, jnp.asarray(level), jnp.asarray(expo, BF16))


def _mix_kernel(z_ref, on_ref, sg_ref, x_ref, wc_ref, wh_ref, wo_ref, gt_ref, g2_ref, sc_ref,
                sh_ref, x1_ref, h2_ref, *, d):
    for r0 in range(0, z_ref.shape[0], MIX_PIECE):
        rows = slice(r0, r0 + MIX_PIECE)
        y_a = _dot(z_ref[rows, :], wc_ref[...])
        y_b = _dot(on_ref[rows, :], wh_ref[...])
        merged = sg_ref[rows, :d].astype(F32) * y_a + sg_ref[rows, d:].astype(F32) * y_b
        x1 = x_ref[rows, :] + gt_ref[...] * _dot(merged.astype(BF16), wo_ref[...])
        x1_ref[rows, :] = x1
        h2_ref[rows, :] = (_rms(x1) * g2_ref[...] * (1.0 + sc_ref[...]) + sh_ref[...]).astype(BF16)


def _mix(z, on, sg, x, wc, wh, wo, gt, g2, sc, sh, seq):
    t, d = x.shape
    tm = TILES["mix_tm"]
    assert tm % MIX_PIECE == 0
    per = seq // tm
    row = lambda i: (i, 0)
    const = lambda i: (0, 0)
    vec = pl.BlockSpec((None, 1, d), lambda i: (i // per, 0, 0))
    resident = lambda shape: pl.BlockSpec(shape, const, pipeline_mode=pl.Buffered(1))
    return pl.pallas_call(
        functools.partial(_mix_kernel, d=d),
        grid=(t // tm,),
        in_specs=[pl.BlockSpec((tm, z.shape[1]), row),
                  pl.BlockSpec((tm, on.shape[1]), row),
                  pl.BlockSpec((tm, 2 * d), row),
                  pl.BlockSpec((tm, d), row),
                  resident(wc.shape), resident(wh.shape), resident(wo.shape),
                  vec, pl.BlockSpec((1, d), const), vec, vec],
        out_specs=[pl.BlockSpec((tm, d), row), pl.BlockSpec((tm, d), row)],
        out_shape=[jax.ShapeDtypeStruct((t, d), F32), jax.ShapeDtypeStruct((t, d), BF16)],
        compiler_params=_cparams("parallel"),
        name="mix",
    )(z, on, sg, x, wc, wh, wo, gt, g2.reshape(1, d), sc, sh)


def _ffn_kernel(h_ref, wg_ref, wu_ref, wd_ref, x1_hbm, gt_ref, gf_ref, o_ref, x1_buf, x1_sem, *, tm):
    i = pl.program_id(0)
    j = pl.program_id(1)

    def x1_copy():
        return pltpu.make_async_copy(x1_hbm.at[pl.ds(i * tm, tm), :], x1_buf, x1_sem)

    halves = [slice(r0, r0 + tm // 2) for r0 in range(0, tm, tm // 2)]

    def down(rows):
        h = h_ref[rows, :]
        act = (_silu(_dot(h, wg_ref[...])) * _dot(h, wu_ref[...])).astype(BF16)
        return _dot(act, wd_ref[...])

    last = pl.num_programs(1) - 1

    @pl.when(j == 0)
    def _():
        x1_copy().start()
        for rows in halves:
            o_ref[rows, :] = down(rows)

    @pl.when(jnp.logical_and(j > 0, j < last))
    def _():
        for rows in halves:
            o_ref[rows, :] += down(rows)

    @pl.when(j == last)
    def _():
        x1_copy().wait()
        for rows in halves:
            x2 = x1_buf[rows, :] + gt_ref[...] * (o_ref[rows, :] + down(rows))
            o_ref[rows, :] = _rms(x2) * gf_ref[...]


def _ffn(h2, wg, wu, wd, x1, gt, gf, seq):
    t, d = x1.shape
    dff = wg.shape[1]
    tm, tf = TILES["ffn_tm"], TILES["ffn_tf"]
    per = seq // tm
    return pl.pallas_call(
        functools.partial(_ffn_kernel, tm=tm),
        grid=(t // tm, dff // tf),
        in_specs=[pl.BlockSpec((tm, d), lambda i, j: (i, 0)),
                  pl.BlockSpec((d, tf), lambda i, j: (0, j)),
                  pl.BlockSpec((d, tf), lambda i, j: (0, j)),
                  pl.BlockSpec((tf, d), lambda i, j: (j, 0)),
                  pl.BlockSpec(memory_space=pl.ANY),
                  pl.BlockSpec((None, 1, d), lambda i, j: (i // per, 0, 0)),
                  pl.BlockSpec((1, d), lambda i, j: (0, 0))],
        out_specs=pl.BlockSpec((tm, d), lambda i, j: (i, 0)),
        out_shape=jax.ShapeDtypeStruct((t, d), F32),
        scratch_shapes=[pltpu.VMEM((tm, d), F32), pltpu.SemaphoreType.DMA(())],
        compiler_params=_cparams("arbitrary", "arbitrary"),
        name="ffn",
    )(h2, wg, wu, wd, x1, gt, gf.reshape(1, d))


def kernel(x, c, w_ada, b_ada, norm_mix_g, w_in, conv_w, lb_param, gnorm_g, w_conv_out,
           w_hgrn_out, w_o, norm_ffn_g, w_ffn_gate, w_ffn_up, w_ffn_down, norm_final_g):
    b, s, d = x.shape
    t = b * s
    depth = w_in.shape[0]
    dc = conv_w.shape[2]
    dk = lb_param.shape[1]
    dv = w_hgrn_out.shape[1]
    assert dk == HEADS * HEAD_DIM and dv == HEADS * HEAD_DIM and s % CHUNK == 0
    tm, tm_conv, tm_norm, tm_gate = (TILES[key] for key in ("proj_tm", "conv_tm", "norm_tm", "gate_tm"))
    cw = TILES["conv_cw"]
    assert dc % cw == 0 and all(s % r == 0 and r % PROJ_PIECE == 0
                                for r in (tm, tm_conv, tm_norm, tm_gate))

    assert depth == 1
    l = 0

    mod = _modulation(c, w_ada[l], b_ada[l]).reshape(b, 6, 1, d)
    sh_m, sc_m, gt_m, sh_f, sc_f, gt_f = [mod[:, i] for i in range(6)]

    w = w_in[l]
    c_q = 3 * dc
    c_f, c_i, c_g, c_gate = c_q + dk, c_q + 2 * dk, c_q + 2 * dk + dv, c_q + 2 * dk + 2 * dv
    tg = TILES["gate_tn"]
    assert (w.shape[1] - c_gate) % tg == 0 and c_gate % tg == 0 and c_q % dk == 0 and dk == dv

    x2d = x.reshape(t, d)
    vec = pl.BlockSpec((None, 1, d), lambda j, i: (i // (s // tm_norm), 0, 0))
    q, h = _proj_call(
        _silu_epilogue, x2d, w, [c_q], dk, 1, [norm_mix_g[l].reshape(1, d), sc_m, sh_m],
        [pl.BlockSpec((1, d), lambda j, i: (0, 0)), vec, vec], [dk, d], [BF16, BF16], tm_norm,
        "proj_q", lhs=_norm_lhs)
    z, wd_b = _proj_call(
        functools.partial(_conv_epilogue, cw=cw, tiles_per_seq=s // tm_conv),
        h, w, [0, dc, 2 * dc], cw, dc // cw, [conv_w[l]],
        [pl.BlockSpec((CONV_WIDTH, cw), lambda j, i: (0, j))], [cw], [BF16], tm_conv, "proj_conv",
        scratch_shapes=[pltpu.VMEM((tm_conv + SUBLANES, cw), F32)], side_casts=[w_ffn_down[l]])
    kk, lf = _proj_call(
        functools.partial(_forget_epilogue, layer=l), h, w, [c_f], dk, 1, [lb_param],
        [pl.BlockSpec((lb_param.shape[0], dk), lambda j, i: (0, 0))],
        [dk, dk], [BF16, F32], tm, "proj_f")
    v, wc_b, wh_b = _proj_call(_ident_epilogue, h, w, [c_i], dv, 1, [], [], [dv], [BF16], tm,
                               "proj_i", side_casts=[w_conv_out[l], w_hgrn_out[l]])
    gs, wo_b = _proj_call(_gate_gain_epilogue, h, w, [c_g], dv, 1,
                          [jnp.tile(gnorm_g[l], HEADS).reshape(1, dv)],
                          [pl.BlockSpec((1, dv), lambda j, i: (0, 0))], [dv], [BF16], tm, "proj_g",
                          side_casts=[w_o[l]])
    sg, wg_b, wu_b = _proj_call(_sigmoid_epilogue, h, w, [c_gate], tg, (w.shape[1] - c_gate) // tg,
                                [], [], [tg], [BF16], tm_gate, "proj_gate",
                                side_casts=[w_ffn_gate[l], w_ffn_up[l]])

    on = _hgrn(q, kk, lf, v, gs, b, s)

    x1, h2 = _mix(z, on, sg, x2d, wc_b, wh_b, wo_b, gt_m, norm_ffn_g[l], sc_f, sh_f, s)
    out = _ffn(h2, wg_b, wu_b, wd_b, x1, gt_f, norm_final_g, s)
    return out.reshape(b, s, d)
```
